```python
import jax
import jax.numpy as jnp
from jax import lax
import numpy as np

D_MODEL = 1024
BATCH = 4
SEQ = 4096
DEPTH = 2
DEC_BATCH = 128
DEC_SEQ = 8
PAST_LEN = 2048
PAGE_SIZE = 128

N_A_LAYERS = DEPTH // 2
N_B_LAYERS = DEPTH - N_A_LAYERS
D_CONV = D_MODEL
CONV_WIDTH = 31
N_HEADS = 16
HEAD_DIM = D_MODEL // N_HEADS
D_ATTN = N_HEADS * HEAD_DIM
MOBA_BLOCK = 256
MOBA_TOPK = 3
Q_CHUNK = 64
NORM_EPS = 1e-6
MASK_VALUE = -1e30

kernel_name = "yoco_conformer_conv_moba_decoder_step"


def rms_norm(x, g):
    x32 = x.astype(jnp.float32)
    y = x32 * lax.rsqrt(jnp.mean(x32 * x32, axis=-1, keepdims=True) + NORM_EPS)
    return (y * g.astype(jnp.float32)).astype(x.dtype)


def layer_norm(x, g, b):
    x32 = x.astype(jnp.float32)
    mu = jnp.mean(x32, axis=-1, keepdims=True)
    xc = x32 - mu
    var = jnp.mean(xc * xc, axis=-1, keepdims=True)
    y = xc * lax.rsqrt(var + NORM_EPS) * g.astype(jnp.float32) + b.astype(jnp.float32)
    return y.astype(x.dtype)


def alibi_slopes():
    return jnp.exp2(-8.0 * jnp.arange(1, N_HEADS + 1, dtype=jnp.float32) / N_HEADS)


def conformer_conv_mixer(x, hist, norm_g, w_in, conv_w, conv_b, ln_g, ln_b, w_out):
    h = rms_norm(x, norm_g)
    a, b, z = jnp.split(h @ w_in, 3, axis=-1)
    glu = a * jax.nn.sigmoid(b)
    full = jnp.concatenate([hist.astype(glu.dtype), glu], axis=1)
    y = lax.conv_general_dilated(
        full, conv_w[:, None, :].astype(full.dtype), window_strides=(1,), padding='VALID',
        dimension_numbers=('NWC', 'WIO', 'NWC'), feature_group_count=D_CONV) + conv_b
    y = jax.nn.silu(layer_norm(y, ln_g, ln_b)) * jax.nn.silu(z)
    return y @ w_out, full[:, full.shape[1] - (CONV_WIDTH - 1):]


def moba_attend_seq(q, k, v, q_pos0):
    sq, seq_len = q.shape[0], k.shape[0]
    nb = -(-seq_len // MOBA_BLOCK)
    kpad = ((0, nb * MOBA_BLOCK - seq_len), (0, 0), (0, 0))
    kp = jnp.pad(k, kpad).reshape(nb, MOBA_BLOCK, N_HEADS, HEAD_DIM)
    vp = jnp.pad(v, kpad).reshape(nb, MOBA_BLOCK, N_HEADS, HEAD_DIM)
    k_mean = jnp.mean(kp.astype(jnp.float32), axis=1)
    k_bh = kp.transpose(2, 0, 1, 3)
    v_bh = vp.transpose(2, 0, 1, 3)
    qc = min(Q_CHUNK, sq)
    n_chunks = -(-sq // qc)
    qp = jnp.pad(q, ((0, n_chunks * qc - sq), (0, 0), (0, 0))).reshape(n_chunks, qc, N_HEADS, HEAD_DIM)
    n_sel = min(MOBA_TOPK, nb)
    slopes = alibi_slopes()
    head_ix = jnp.arange(N_HEADS)[None, :, None]
    offs = jnp.arange(MOBA_BLOCK)
    scale = HEAD_DIM ** -0.5

    def chunk(args):
        qb, c = args
        pos0 = q_pos0 + c * qc
        t = pos0 + jnp.arange(qc)
        cur = pos0 // MOBA_BLOCK
        scores = jnp.einsum('qhd,nhd->qhn', qb.astype(jnp.float32), k_mean)
        scores = jnp.where(jnp.arange(nb) < cur, scores, MASK_VALUE)
        _, idx = lax.top_k(scores, n_sel)
        valid = idx < cur
        k_sel = k_bh[head_ix, idx]
        v_sel = v_bh[head_ix, idx]
        lg_sel = jnp.einsum('qhd,qhkrd->qhkr', qb, k_sel, preferred_element_type=jnp.float32) * scale
        s_sel = idx[..., None] * MOBA_BLOCK + offs
        d_sel = (t[:, None, None, None] - s_sel).astype(jnp.float32)
        lg_sel = jnp.where(valid[..., None], lg_sel - slopes[None, :, None, None] * d_sel, MASK_VALUE)
        k_own = lax.dynamic_index_in_dim(kp, cur, axis=0, keepdims=False)
        v_own = lax.dynamic_index_in_dim(vp, cur, axis=0, keepdims=False)
        lg_own = jnp.einsum('qhd,rhd->qhr', qb, k_own, preferred_element_type=jnp.float32) * scale
        d_own = (t[:, None] - (cur * MOBA_BLOCK + offs)[None, :]).astype(jnp.float32)
        lg_own = jnp.where(d_own[:, None, :] >= 0.0, lg_own - slopes[None, :, None] * d_own[:, None, :], MASK_VALUE)
        logits = jnp.concatenate([lg_sel.reshape(qc, N_HEADS, n_sel * MOBA_BLOCK), lg_own], axis=-1)
        p = jax.nn.softmax(logits, axis=-1).astype(v.dtype)
        p_sel = p[..., :n_sel * MOBA_BLOCK].reshape(qc, N_HEADS, n_sel, MOBA_BLOCK)
        p_own = p[..., n_sel * MOBA_BLOCK:]
        return (jnp.einsum('qhkr,qhkrd->qhd', p_sel, v_sel)
                + jnp.einsum('qhr,rhd->qhd', p_own, v_own))

    out = lax.map(chunk, (qp, jnp.arange(n_chunks)))
    return out.reshape(n_chunks * qc, N_HEADS, HEAD_DIM)[:sq]


def moba_layer(x, k, v, attend, norm_g, w_in, w_out):
    n, s, _ = x.shape
    h = rms_norm(x, norm_g)
    q, z = jnp.split(h @ w_in, 2, axis=-1)
    o = attend(q.reshape(n, s, N_HEADS, HEAD_DIM), k, v)
    return (o.reshape(n, s, D_ATTN) * jax.nn.silu(z)) @ w_out


def setup_inputs(seed: int = 0) -> dict:
    key = jax.random.key(seed)
    ks = jax.random.split(key, 24)
    f32 = jnp.float32

    def nrm(k, shape, scale):
        return jax.random.normal(k, shape, f32) * scale

    n_pages = PAST_LEN // PAGE_SIZE
    n_used = DEC_BATCH * n_pages
    n_phys = n_used + max(1, n_used // 4)
    page_table = jax.random.permutation(ks[0], n_phys)[:n_used].reshape(DEC_BATCH, n_pages).astype(jnp.int32)
    return {
        'x_prompt': nrm(ks[1], (BATCH, SEQ, D_MODEL), 1.0),
        'x_sample': nrm(ks[2], (DEC_BATCH, DEC_SEQ, D_MODEL), 1.0),
        'state_conv': nrm(ks[3], (N_A_LAYERS, DEC_BATCH, CONV_WIDTH - 1, D_CONV), 0.5),
        'cache_k': nrm(ks[4], (n_phys, PAGE_SIZE, N_HEADS, HEAD_DIM), 1.0),
        'cache_v': nrm(ks[5], (n_phys, PAGE_SIZE, N_HEADS, HEAD_DIM), 1.0),
        'page_table': page_table,
        'a_norm_g': 1.0 + nrm(ks[6], (N_A_LAYERS, D_MODEL), 0.02),
        'a_w_in': nrm(ks[7], (N_A_LAYERS, D_MODEL, 3 * D_CONV), D_MODEL ** -0.5),
        'a_conv_w': nrm(ks[8], (N_A_LAYERS, CONV_WIDTH, D_CONV), CONV_WIDTH ** -0.5),
        'a_conv_b': nrm(ks[9], (N_A_LAYERS, D_CONV), 0.02),
        'a_ln_g': 1.0 + nrm(ks[10], (N_A_LAYERS, D_CONV), 0.02),
        'a_ln_b': nrm(ks[11], (N_A_LAYERS, D_CONV), 0.02),
        'a_w_out': nrm(ks[12], (N_A_LAYERS, D_CONV, D_MODEL), D_CONV ** -0.5),
        'kv_norm_g': 1.0 + nrm(ks[13], (D_MODEL,), 0.02),
        'w_kv': nrm(ks[14], (D_MODEL, 2 * D_ATTN), D_MODEL ** -0.5),
        'b_norm_g': 1.0 + nrm(ks[15], (N_B_LAYERS, D_MODEL), 0.02),
        'b_w_in': nrm(ks[16], (N_B_LAYERS, D_MODEL, 2 * D_ATTN), D_MODEL ** -0.5),
        'b_w_out': nrm(ks[17], (N_B_LAYERS, D_ATTN, D_MODEL), D_ATTN ** -0.5),
        'final_norm_g': 1.0 + nrm(ks[18], (D_MODEL,), 0.02),
    }


def reference(x_prompt, x_sample, state_conv, cache_k, cache_v, page_table,
              a_norm_g, a_w_in, a_conv_w, a_conv_b, a_ln_g, a_ln_b, a_w_out,
              kv_norm_g, w_kv, b_norm_g, b_w_in, b_w_out, final_norm_g):

    def trunk(x, conv_hist, attend):
        hists = []
        k = v = None
        for layer in range(DEPTH):
            if layer < N_A_LAYERS:
                dy, h_new = conformer_conv_mixer(
                    x, conv_hist[layer], a_norm_g[layer], a_w_in[layer], a_conv_w[layer],
                    a_conv_b[layer], a_ln_g[layer], a_ln_b[layer], a_w_out[layer])
                x = x + dy
                hists.append(h_new)
                if layer == N_A_LAYERS - 1:
                    n, s, _ = x.shape
                    k, v = jnp.split(rms_norm(x, kv_norm_g) @ w_kv, 2, axis=-1)
                    k = k.reshape(n, s, N_HEADS, HEAD_DIM)
                    v = v.reshape(n, s, N_HEADS, HEAD_DIM)
            else:
                j = layer - N_A_LAYERS
                x = x + moba_layer(x, k, v, attend, b_norm_g[j], b_w_in[j], b_w_out[j])
        return rms_norm(x, final_norm_g), jnp.stack(hists), k, v

    def attend_prompt(q, k, v):
        return lax.map(lambda a: moba_attend_seq(a[0], a[1], a[2], 0), (q, k, v))

    def attend_sample(q, k_new, v_new):
        def one(a):
            qb, kb, vb, pages = a
            k_past = cache_k[pages].reshape(-1, N_HEADS, HEAD_DIM)
            v_past = cache_v[pages].reshape(-1, N_HEADS, HEAD_DIM)
            k_full = jnp.concatenate([k_past, kb.astype(k_past.dtype)], axis=0)
            v_full = jnp.concatenate([v_past, vb.astype(v_past.dtype)], axis=0)
            return moba_attend_seq(qb, k_full, v_full, k_past.shape[0])
        return lax.map(one, (q, k_new, v_new, page_table))

    zero_hist = jnp.zeros((N_A_LAYERS, x_prompt.shape[0], CONV_WIDTH - 1, D_CONV), x_prompt.dtype)
    y_prompt, conv_state_prompt, k_prompt, v_prompt = trunk(x_prompt, zero_hist, attend_prompt)
    y_sample, conv_state_sample, k_sample, v_sample = trunk(x_sample, state_conv, attend_sample)
    return (y_prompt, y_sample, conv_state_prompt, conv_state_sample, k_prompt, v_prompt, k_sample, v_sample)
```

```python
import functools

import jax
import jax.numpy as jnp
from jax import lax
from jax.experimental import pallas as pl
from jax.experimental.pallas import tpu as pltpu

D_MODEL = 1024
N_HEADS = 16
HEAD_DIM = 64
CONV_WIDTH = 31
HIST = CONV_WIDTH - 1
MOBA_BLOCK = 256
MOBA_TOPK = 3
PAGE_SIZE = 128
NORM_EPS = 1e-6
MASK_VALUE = -1e30

LANES = 128
HEADS_PER_VREG = LANES // HEAD_DIM
N_HEAD_PAIRS = N_HEADS // HEADS_PER_VREG
VMEM_LIMIT = 56 * 1024 * 1024

F32 = jnp.float32
BF16 = jnp.bfloat16


def _idiv(x, n):
    assert n & (n - 1) == 0
    return lax.shift_right_logical(x, n.bit_length() - 1)


def _imod(x, n):
    assert n & (n - 1) == 0
    return lax.bitwise_and(x, n - 1)


def _sigmoid(x):
    return 1.0 / (1.0 + jnp.exp(-x))


def _dot(a, b):
    return jnp.dot(a, b, preferred_element_type=F32)


def _dot_nt(a, b, precision=None):
    return lax.dot_general(a, b, (((1,), (1,)), ((), ())),
                           preferred_element_type=F32, precision=precision)


def _params(*semantics):
    return pltpu.CompilerParams(dimension_semantics=semantics,
                                vmem_limit_bytes=VMEM_LIMIT)


def _const_spec(shape):
    zeros = (0,) * len(shape)
    return pl.BlockSpec(shape, lambda *_: zeros, pipeline_mode=pl.Buffered(1))


def _inproj_body(x_ref, g_ref, w_ref, glu_ref, sz_ref):
    x = x_ref[...]
    h = x * lax.rsqrt(jnp.mean(x * x, axis=-1, keepdims=True) + NORM_EPS) * g_ref[...]
    hb = h.astype(BF16)
    a = _dot(hb, w_ref[:, 0:D_MODEL])
    b = _dot(hb, w_ref[:, D_MODEL:2 * D_MODEL])
    z = _dot(hb, w_ref[:, 2 * D_MODEL:3 * D_MODEL])
    glu_ref[...] = a * _sigmoid(b)
    sz_ref[...] = z * _sigmoid(z)


def _inproj(x, norm_g, w_in_bf, rows_per_step):
    n_rows = x.shape[0]
    row_spec = pl.BlockSpec((rows_per_step, D_MODEL), lambda i: (i, 0))
    return pl.pallas_call(
        _inproj_body,
        grid=(n_rows // rows_per_step,),
        in_specs=[row_spec, _const_spec((1, D_MODEL)), _const_spec((D_MODEL, 3 * D_MODEL))],
        out_specs=[row_spec, row_spec],
        out_shape=[jax.ShapeDtypeStruct((n_rows, D_MODEL), F32)] * 2,
        compiler_params=_params("arbitrary"),
        name="inproj",
    )(x, norm_g, w_in_bf)


def _post_conv(y, sz, x, ln_g, ln_b, w_out_ref, kv_g, b_g, bw_in_ref):
    mu = jnp.mean(y, axis=-1, keepdims=True)
    yc = y - mu
    var = jnp.mean(yc * yc, axis=-1, keepdims=True)
    ln = yc * lax.rsqrt(var + NORM_EPS) * ln_g + ln_b
    u = ln * _sigmoid(ln) * sz
    x1 = x + _dot(u.astype(BF16), w_out_ref[...])
    xn = x1 * lax.rsqrt(jnp.mean(x1 * x1, axis=-1, keepdims=True) + NORM_EPS)
    qz = _dot((xn * b_g).astype(BF16), bw_in_ref[...])
    z2 = qz[:, D_MODEL:]
    return x1, qz[:, :D_MODEL], z2 * _sigmoid(z2), (xn * kv_g).astype(BF16)


_MID_ROWS = MOBA_BLOCK
_CARRY = 32
_CONV_ROWS = 128


def _mid_prompt_body(glu_ref, sz_ref, x_ref, cw_ref, cb_ref, lng_ref, lnb_ref, wout_ref,
                     kvg_ref, wkvt_ref, bg_ref, bwin_ref,
                     x1_ref, q_ref, sz2_ref, kt_ref, vt_ref, ktb_ref, vtb_ref, kmt_ref,
                     full_ref, y_ref):
    t = pl.program_id(1)

    @pl.when(t == 0)
    def _():
        full_ref[0:_CARRY, :] = jnp.zeros((_CARRY, D_MODEL), F32)

    @pl.when(t > 0)
    def _():
        full_ref[0:_CARRY, :] = full_ref[_MID_ROWS:_MID_ROWS + _CARRY, :]

    full_ref[_CARRY:_CARRY + _MID_ROWS, :] = glu_ref[0]

    def lane_group(c, carry):
        off = pl.multiple_of(c * LANES, LANES)
        for rc in range(_MID_ROWS // _CONV_ROWS):
            acc = jnp.zeros((_CONV_ROWS, LANES), F32)
            for w in range(CONV_WIDTH):
                r0 = _CARRY - HIST + w + rc * _CONV_ROWS
                acc = acc + full_ref[pl.ds(r0, _CONV_ROWS), pl.ds(off, LANES)] * cw_ref[w:w + 1, pl.ds(off, LANES)]
            y_ref[pl.ds(rc * _CONV_ROWS, _CONV_ROWS), pl.ds(off, LANES)] = acc + cb_ref[:, pl.ds(off, LANES)]
        return carry

    lax.fori_loop(0, D_MODEL // LANES, lane_group, 0)

    x1, q, sz2, xkv = _post_conv(y_ref[...], sz_ref[0], x_ref[0], lng_ref[...], lnb_ref[...], wout_ref,
                                 kvg_ref[...], bg_ref[...], bwin_ref)
    x1_ref[0] = x1
    q_ref[0] = q
    sz2_ref[0] = sz2
    kvt = _dot_nt(wkvt_ref[...], xkv)
    kt = kvt[:D_MODEL]
    vt = kvt[D_MODEL:]
    kt_ref[0] = kt
    vt_ref[0] = vt
    ktb_ref[0] = kt.astype(BF16)
    vtb_ref[0] = vt.astype(BF16)
    mean_col = jnp.sum(kt, axis=1, keepdims=True) * (1.0 / MOBA_BLOCK)
    lane = lax.broadcasted_iota(jnp.int32, (D_MODEL, LANES), 1)

    @pl.when(t == 0)
    def _():
        kmt_ref[0] = jnp.where(lane == 0, mean_col, 0.0)

    @pl.when(t > 0)
    def _():
        kmt_ref[0] = jnp.where(lane == t, mean_col, kmt_ref[0])


def _mid_prompt(glu, sz, x, cw, cb, ln_g, ln_b, w_out_bf, kv_g, w_kvt_bf, b_g, bw_in_bf):
    n_seq, seq, _ = x.shape
    n_tiles = seq // _MID_ROWS
    assert n_tiles <= LANES
    tile = pl.BlockSpec((1, _MID_ROWS, D_MODEL), lambda b, t: (b, t, 0))
    tile_t = pl.BlockSpec((1, D_MODEL, _MID_ROWS), lambda b, t: (b, 0, t))
    vec = _const_spec((1, D_MODEL))
    f32_rows = jax.ShapeDtypeStruct((n_seq, seq, D_MODEL), F32)
    return pl.pallas_call(
        _mid_prompt_body,
        grid=(n_seq, n_tiles),
        in_specs=[tile, tile, tile, _const_spec((CONV_WIDTH, D_MODEL)), vec, vec, vec,
                  _const_spec((D_MODEL, D_MODEL)), vec, _const_spec((2 * D_MODEL, D_MODEL)),
                  vec, _const_spec((D_MODEL, 2 * D_MODEL))],
        out_specs=[tile] * 3 + [tile_t] * 4 + [pl.BlockSpec((1, D_MODEL, LANES), lambda b, t: (b, 0, 0))],
        out_shape=[f32_rows] * 3
                  + [jax.ShapeDtypeStruct((n_seq, D_MODEL, seq), F32)] * 2
                  + [jax.ShapeDtypeStruct((n_seq, D_MODEL, seq), BF16)] * 2
                  + [jax.ShapeDtypeStruct((n_seq, D_MODEL, LANES), F32)],
        scratch_shapes=[pltpu.VMEM((_CARRY + _MID_ROWS, D_MODEL), F32),
                        pltpu.VMEM((_MID_ROWS, D_MODEL), F32)],
        compiler_params=_params("arbitrary", "arbitrary"),
        name="mid_prompt",
    )(glu, sz, x, cw, cb, ln_g, ln_b, w_out_bf, kv_g, w_kvt_bf, b_g, bw_in_bf)


def _conv_sample_body(dec_seq, hist_ref, glu_ref, cw_ref, cb_ref, y_ref):
    def tap(r):
        return hist_ref[r] if r < HIST else glu_ref[r - HIST]

    for t in range(dec_seq):
        acc = tap(t) * cw_ref[0:1, :]
        for w in range(1, CONV_WIDTH):
            acc = acc + tap(t + w) * cw_ref[w:w + 1, :]
        y_ref[t] = acc + cb_ref[...]


def _conv_sample(hist_t, glu_t, cw, cb):
    dec_seq, n_seq, _ = glu_t.shape
    lane_blk = lambda rows: pl.BlockSpec((rows, n_seq, LANES), lambda c: (0, 0, c))
    return pl.pallas_call(
        functools.partial(_conv_sample_body, dec_seq),
        grid=(D_MODEL // LANES,),
        in_specs=[lane_blk(HIST), lane_blk(dec_seq),
                  pl.BlockSpec((CONV_WIDTH, LANES), lambda c: (0, c)),
                  pl.BlockSpec((1, LANES), lambda c: (0, c))],
        out_specs=lane_blk(dec_seq),
        out_shape=jax.ShapeDtypeStruct((dec_seq, n_seq, D_MODEL), F32),
        compiler_params=_params("arbitrary"),
        name="conv_sample",
    )(hist_t, glu_t, cw, cb)


def _post_sample_body(t_per_step, n_seq, y_ref, sz_ref, x_ref, lng_ref, lnb_ref, wout_ref,
                      kvg_ref, wkv_ref, wkvt_ref, bg_ref, bwin_ref,
                      x1_ref, q_ref, sz2_ref, k_ref, v_ref, kt_ref, vt_ref):
    x1, q, sz2, xkv = _post_conv(y_ref[...], sz_ref[...], x_ref[...], lng_ref[...], lnb_ref[...], wout_ref,
                                 kvg_ref[...], bg_ref[...], bwin_ref)
    x1_ref[...] = x1
    q_ref[...] = q
    sz2_ref[...] = sz2
    kv = _dot(xkv, wkv_ref[...])
    k_ref[...] = kv[:, :D_MODEL]
    v_ref[...] = kv[:, D_MODEL:]
    kvt = _dot_nt(wkvt_ref[...], xkv)
    for i in range(t_per_step):
        kt_ref[i] = kvt[:D_MODEL, i * n_seq:(i + 1) * n_seq]
        vt_ref[i] = kvt[D_MODEL:, i * n_seq:(i + 1) * n_seq]


def _post_sample(y, sz, x, ln_g, ln_b, w_out_bf, kv_g, w_kv_bf, w_kvt_bf, b_g, bw_in_bf, n_seq, rows_per_step):
    n_rows = y.shape[0]
    dec_seq = n_rows // n_seq
    t_per_step = rows_per_step // n_seq
    assert n_seq % LANES == 0 and rows_per_step % n_seq == 0
    row_spec = pl.BlockSpec((rows_per_step, D_MODEL), lambda i: (i, 0))
    t_spec = pl.BlockSpec((t_per_step, D_MODEL, n_seq), lambda i: (i, 0, 0))
    vec = _const_spec((1, D_MODEL))
    f32_rows = jax.ShapeDtypeStruct((n_rows, D_MODEL), F32)
    f32_t = jax.ShapeDtypeStruct((dec_seq, D_MODEL, n_seq), F32)
    return pl.pallas_call(
        functools.partial(_post_sample_body, t_per_step, n_seq),
        grid=(n_rows // rows_per_step,),
        in_specs=[row_spec, row_spec, row_spec, vec, vec, _const_spec((D_MODEL, D_MODEL)), vec,
                  _const_spec((D_MODEL, 2 * D_MODEL)), _const_spec((2 * D_MODEL, D_MODEL)),
                  vec, _const_spec((D_MODEL, 2 * D_MODEL))],
        out_specs=[row_spec] * 5 + [t_spec] * 2,
        out_shape=[f32_rows] * 5 + [f32_t] * 2,
        compiler_params=_params("arbitrary"),
        name="post_sample",
    )(y, sz, x, ln_g, ln_b, w_out_bf, kv_g, w_kv_bf, w_kvt_bf, b_g, bw_in_bf)


def _select_body(n_blocks, q_ref, kmt_ref, bias_ref):
    cur = pl.program_id(1)
    q = q_ref[0]
    km = kmt_ref[0].T[0:n_blocks]
    lane_head = _idiv(lax.broadcasted_iota(jnp.int32, (n_blocks, D_MODEL), 1), HEAD_DIM)
    km_heads = jnp.concatenate([jnp.where(lane_head == h, km, 0.0) for h in range(N_HEADS)], axis=0)
    scores = _dot_nt(km_heads, q, precision=lax.Precision.HIGHEST)
    blk = lax.broadcasted_iota(jnp.int32, (n_blocks, MOBA_BLOCK), 0)
    past = blk < cur
    pieces = []
    for h in range(N_HEADS):
        s = jnp.where(past, scores[h * n_blocks:(h + 1) * n_blocks, :], MASK_VALUE)
        rank = jnp.zeros((n_blocks, MOBA_BLOCK), jnp.int32)
        for j in range(n_blocks):
            sj = s[j:j + 1, :]
            ahead = jnp.where(sj > s, 1, jnp.where(sj == s, jnp.where(blk > j, 1, 0), 0))
            rank = rank + ahead
        pieces.append(jnp.where(jnp.where(past, rank, MOBA_TOPK) < MOBA_TOPK, 0.0, MASK_VALUE))
    pad = jnp.zeros((LANES - HEADS_PER_VREG * n_blocks, MOBA_BLOCK), F32)
    for hp in range(N_HEAD_PAIRS):
        rows = jnp.concatenate(pieces[hp * HEADS_PER_VREG:(hp + 1) * HEADS_PER_VREG] + [pad], axis=0)
        bias_ref[0, hp] = rows.T.astype(BF16)


def _select(q, kmean_t):
    n_seq, seq, _ = q.shape
    n_blocks = seq // MOBA_BLOCK
    assert HEADS_PER_VREG * n_blocks <= LANES and n_blocks % 8 == 0
    return pl.pallas_call(
        functools.partial(_select_body, n_blocks),
        grid=(n_seq, n_blocks),
        in_specs=[pl.BlockSpec((1, MOBA_BLOCK, D_MODEL), lambda b, i: (b, i, 0)),
                  pl.BlockSpec((1, D_MODEL, LANES), lambda b, i: (b, 0, 0))],
        out_specs=pl.BlockSpec((1, N_HEAD_PAIRS, MOBA_BLOCK, LANES), lambda b, i: (b, 0, i, 0)),
        out_shape=jax.ShapeDtypeStruct((n_seq, N_HEAD_PAIRS, seq, LANES), BF16),
        compiler_params=_params("arbitrary", "arbitrary"),
        name="moba_select",
    )(q, kmean_t)


def _softmax_step(t, shift, vt_bf, m_ref, l_ref, acc_ref, e, first):
    n_rep = t.shape[1] // LANES
    m_cur = jnp.max(t, axis=1, keepdims=True) - shift
    if first:
        m_new = jnp.broadcast_to(m_cur, (t.shape[0], LANES))
    else:
        m_prev = m_ref[e]
        m_new = jnp.maximum(m_prev, m_cur)
    p = jnp.exp(t - pltpu.repeat(m_new + shift, n_rep, axis=1))
    row_sum = jnp.sum(p, axis=1, keepdims=True)
    pv = _dot_nt(p.astype(BF16), vt_bf)
    if first:
        l_ref[e] = jnp.broadcast_to(row_sum, (t.shape[0], LANES))
        acc_ref[e] = pv
    else:
        alpha = jnp.exp(m_prev - m_new)
        l_ref[e] = alpha * l_ref[e] + row_sum
        acc_ref[e] = alpha * acc_ref[e] + pv
    m_ref[e] = m_new


def _attend_prompt_body(n_blocks, slopes_ref, q_ref, bias_ref, kt_ref, vt_ref, o_ref,
                        qaug_ref, sd_ref, m_ref, l_ref, acc_ref):
    hp = pl.program_id(1)
    i = pl.program_id(2)
    q2 = q_ref[0] * (HEAD_DIM ** -0.5)
    tail = bias_ref[0, 0].astype(F32)
    lane = lax.broadcasted_iota(jnp.int32, (MOBA_BLOCK, LANES), 1)
    row = lax.broadcasted_iota(jnp.int32, (MOBA_BLOCK, MOBA_BLOCK), 0)
    col = lax.broadcasted_iota(jnp.int32, (MOBA_BLOCK, MOBA_BLOCK), 1)
    dist = (row - col).astype(F32)
    own0 = pl.multiple_of(i * MOBA_BLOCK, MOBA_BLOCK)
    kt_own = kt_ref[0, :, pl.ds(own0, MOBA_BLOCK)]
    vt_own = vt_ref[0, :, pl.ds(own0, MOBA_BLOCK)]
    for e in range(HEADS_PER_VREG):
        slope = slopes_ref[hp * HEADS_PER_VREG + e]
        q_e = jnp.where(_idiv(lane, HEAD_DIM) == e, q2, 0.0).astype(BF16)
        tail_e = jnp.where(_idiv(lane, n_blocks) == e, tail, 0.0).astype(BF16)
        qaug_ref[e] = jnp.concatenate([q_e, tail_e], axis=1)
        sd = slope * dist
        sd_ref[e] = sd
        t = jnp.where(row >= col, _dot(q_e, kt_own) - sd, MASK_VALUE)
        _softmax_step(t, 0.0, vt_own, m_ref, l_ref, acc_ref, e, first=True)

    brow = lax.broadcasted_iota(jnp.int32, (LANES, MOBA_BLOCK), 0)
    brow_blk = jnp.where(brow < HEADS_PER_VREG * n_blocks, _imod(brow, n_blocks), -1)

    def past_block(j, carry):
        off = pl.multiple_of(j * MOBA_BLOCK, MOBA_BLOCK)
        kt_j = kt_ref[0, :, pl.ds(off, MOBA_BLOCK)]
        vt_j = vt_ref[0, :, pl.ds(off, MOBA_BLOCK)]
        marker = jnp.where(brow_blk == j, 1.0, 0.0).astype(BF16)
        kt_aug = jnp.concatenate([kt_j, marker], axis=0)
        gap = ((i - j) * MOBA_BLOCK).astype(F32)
        for e in range(HEADS_PER_VREG):
            slope = slopes_ref[hp * HEADS_PER_VREG + e]
            t = _dot(qaug_ref[e], kt_aug) - sd_ref[e]
            _softmax_step(t, slope * gap, vt_j, m_ref, l_ref, acc_ref, e, first=False)
        return carry

    lax.fori_loop(0, i, past_block, 0)

    out = [acc_ref[e] / l_ref[e] for e in range(HEADS_PER_VREG)]
    o_ref[0] = jnp.where(lane < HEAD_DIM, out[0], out[1])


def _attend_prompt(q, bias, kt_bf, vt_bf, slopes):
    n_seq, seq, _ = q.shape
    n_blocks = seq // MOBA_BLOCK
    assert HEADS_PER_VREG == 2
    q_spec = pl.BlockSpec((1, MOBA_BLOCK, LANES), lambda b, hp, i: (b, i, hp))
    kv_spec = pl.BlockSpec((1, LANES, seq), lambda b, hp, i: (b, hp, 0))
    stats = pltpu.VMEM((HEADS_PER_VREG, MOBA_BLOCK, LANES), F32)
    return pl.pallas_call(
        functools.partial(_attend_prompt_body, n_blocks),
        grid=(n_seq, N_HEAD_PAIRS, n_blocks),
        in_specs=[pl.BlockSpec(memory_space=pltpu.SMEM), q_spec,
                  pl.BlockSpec((1, 1, MOBA_BLOCK, LANES), lambda b, hp, i: (b, hp, i, 0)),
                  kv_spec, kv_spec],
        out_specs=q_spec,
        out_shape=jax.ShapeDtypeStruct((n_seq, seq, D_MODEL), F32),
        scratch_shapes=[pltpu.VMEM((HEADS_PER_VREG, MOBA_BLOCK, 2 * LANES), BF16),
                        pltpu.VMEM((HEADS_PER_VREG, MOBA_BLOCK, MOBA_BLOCK), F32),
                        stats, stats, stats],
        compiler_params=_params("arbitrary", "arbitrary", "arbitrary"),
        name="moba_attend_prompt",
    )(slopes, q, bias, kt_bf, vt_bf)


def _attend_sample_body(dec_seq, n_past, past_len,
                        pt_ref, q_ref, slope_ref, k0_ref, k1_ref, v0_ref, v1_ref, kn_ref, vn_ref, o_ref,
                        qdt_ref, qdb_ref, sc_ref, m_ref, l_ref, acc_ref):
    j = pl.program_id(1)
    rows = N_HEADS * dec_seq
    lane = lax.broadcasted_iota(jnp.int32, (rows, LANES), 1)
    slope = slope_ref[...]

    @pl.when(j == 0)
    def _():
        q = q_ref[0]
        q_rows = jnp.concatenate([q] * N_HEADS, axis=0)
        r_head = _idiv(lax.broadcasted_iota(jnp.int32, (rows, D_MODEL), 0), dec_seq)
        l_head = _idiv(lax.broadcasted_iota(jnp.int32, (rows, D_MODEL), 1), HEAD_DIM)
        q_diag = jnp.where(r_head == l_head, q_rows, 0.0)
        qdt_ref[...] = q_diag.T
        qdb_ref[...] = (q_diag * (HEAD_DIM ** -0.5)).astype(BF16)

    kt_blk = jnp.concatenate([k0_ref[0], k1_ref[0]], axis=1)
    vt_blk = jnp.concatenate([v0_ref[0], v1_ref[0]], axis=1)
    k_mean = jnp.sum(kt_blk, axis=1, keepdims=True) * (1.0 / MOBA_BLOCK)
    sc_ref[pl.ds(j, 1), :] = jnp.sum(qdt_ref[...] * k_mean, axis=0, keepdims=True)
    s = _dot(qdb_ref[...], kt_blk.astype(BF16))
    q_pos = past_len + _imod(lax.broadcasted_iota(jnp.int32, (rows, MOBA_BLOCK), 0), dec_seq)
    k_pos = j * MOBA_BLOCK + lax.broadcasted_iota(jnp.int32, (rows, MOBA_BLOCK), 1)
    t = s - slope * (q_pos - k_pos).astype(F32)
    m_j = jnp.max(t, axis=1, keepdims=True)
    p = jnp.exp(t - m_j)
    l_j = jnp.sum(p, axis=1, keepdims=True)
    acc_ref[j] = _dot_nt(p.astype(BF16), vt_blk.astype(BF16))

    @pl.when(j == 0)
    def _():
        m_ref[...] = jnp.broadcast_to(m_j, (rows, LANES))
        l_ref[...] = jnp.broadcast_to(l_j, (rows, LANES))

    @pl.when(j > 0)
    def _():
        m_ref[...] = jnp.where(lane == j, m_j, m_ref[...])
        l_ref[...] = jnp.where(lane == j, l_j, l_ref[...])

    @pl.when(j == n_past - 1)
    def _():
        pad = jnp.zeros((LANES - dec_seq, D_MODEL), F32)
        k_new = jnp.concatenate([kn_ref[0], pad], axis=0).astype(BF16)
        v_new = jnp.concatenate([vn_ref[0], pad], axis=0).astype(BF16)
        s_own = _dot_nt(qdb_ref[...], k_new)
        d_own = _imod(lax.broadcasted_iota(jnp.int32, (rows, LANES), 0), dec_seq) - lane
        t_own = jnp.where(d_own >= 0, s_own - slope * d_own.astype(F32), MASK_VALUE)
        m_own = jnp.max(t_own, axis=1, keepdims=True)
        p_own = jnp.exp(t_own - m_own)
        l_own = jnp.sum(p_own, axis=1, keepdims=True)
        acc_own = _dot(p_own.astype(BF16), v_new)

        sc = sc_ref[...]
        blk = lax.broadcasted_iota(jnp.int32, (n_past, rows), 0)
        rank = jnp.zeros((n_past, rows), jnp.int32)
        for jj in range(n_past):
            sj = sc[jj:jj + 1, :]
            ahead = jnp.where(sj > sc, 1, jnp.where(sj == sc, jnp.where(blk > jj, 1, 0), 0))
            rank = rank + ahead
        chosen_t = jnp.where(rank < min(MOBA_TOPK, n_past), 1.0, 0.0)
        chosen = jnp.concatenate([chosen_t, jnp.zeros((LANES - n_past, rows), F32)], axis=0).T > 0.5

        m_all = m_ref[...]
        m_top = jnp.maximum(jnp.max(jnp.where(chosen, m_all, MASK_VALUE), axis=1, keepdims=True), m_own)
        w_all = jnp.where(chosen, jnp.exp(m_all - m_top), 0.0)
        w_own = jnp.exp(m_own - m_top)
        denom = jnp.sum(w_all * l_ref[...], axis=1, keepdims=True) + w_own * l_own
        num = w_own * acc_own
        for jj in range(n_past):
            num = num + w_all[:, jj:jj + 1] * acc_ref[jj]
        out = num / denom
        l_head = _idiv(lax.broadcasted_iota(jnp.int32, (dec_seq, D_MODEL), 1), HEAD_DIM)
        res = jnp.zeros((dec_seq, D_MODEL), F32)
        for h in range(N_HEADS):
            res = jnp.where(l_head == h, out[h * dec_seq:(h + 1) * dec_seq, :], res)
        o_ref[0] = res


def _attend_sample(q, k_new, v_new, cache_kt, cache_vt, page_table, slope_rows):
    n_seq, dec_seq, _ = q.shape
    n_pages = page_table.shape[1]
    pages_per_block = MOBA_BLOCK // PAGE_SIZE
    assert pages_per_block == 2 and n_pages % pages_per_block == 0 and dec_seq % 8 == 0
    n_past = n_pages // pages_per_block
    past_len = n_pages * PAGE_SIZE
    rows = N_HEADS * dec_seq
    assert n_past % 8 == 0 and n_past <= LANES and dec_seq <= LANES and rows % LANES == 0
    new_spec = pl.BlockSpec((1, dec_seq, D_MODEL), lambda b, j, pt: (b, 0, 0))

    def page_spec(which):
        return pl.BlockSpec((1, D_MODEL, PAGE_SIZE), lambda b, j, pt: (pt[b, pages_per_block * j + which], 0, 0))

    stats = pltpu.VMEM((rows, LANES), F32)
    grid_spec = pltpu.PrefetchScalarGridSpec(
        num_scalar_prefetch=1,
        grid=(n_seq, n_past),
        in_specs=[new_spec, pl.BlockSpec((rows, 1), lambda b, j, pt: (0, 0)),
                  page_spec(0), page_spec(1), page_spec(0), page_spec(1), new_spec, new_spec],
        out_specs=new_spec,
        scratch_shapes=[pltpu.VMEM((D_MODEL, rows), F32), pltpu.VMEM((rows, D_MODEL), BF16),
                        pltpu.VMEM((n_past, rows), F32), stats, stats,
                        pltpu.VMEM((n_past, rows, D_MODEL), F32)],
    )
    return pl.pallas_call(
        functools.partial(_attend_sample_body, dec_seq, n_past, past_len),
        grid_spec=grid_spec,
        out_shape=jax.ShapeDtypeStruct((n_seq, dec_seq, D_MODEL), F32),
        compiler_params=_params("arbitrary", "arbitrary"),
        name="moba_attend_sample",
    )(page_table, q, slope_rows, cache_kt, cache_kt, cache_vt, cache_vt, k_new, v_new)


def _out_body(o_ref, sz2_ref, x1_ref, w_ref, g_ref, y_ref):
    x2 = x1_ref[...] + _dot((o_ref[...] * sz2_ref[...]).astype(BF16), w_ref[...])
    y_ref[...] = x2 * lax.rsqrt(jnp.mean(x2 * x2, axis=-1, keepdims=True) + NORM_EPS) * g_ref[...]


def _out(o, sz2, x1, w_out_bf, final_g, rows_per_step):
    n_rows = o.shape[0]
    row_spec = pl.BlockSpec((rows_per_step, D_MODEL), lambda i: (i, 0))
    return pl.pallas_call(
        _out_body,
        grid=(n_rows // rows_per_step,),
        in_specs=[row_spec, row_spec, row_spec, _const_spec((D_MODEL, D_MODEL)), _const_spec((1, D_MODEL))],
        out_specs=row_spec,
        out_shape=jax.ShapeDtypeStruct((n_rows, D_MODEL), F32),
        compiler_params=_params("arbitrary"),
        name="out",
    )(o, sz2, x1, w_out_bf, final_g)


_ROWS_PER_STEP = 256


def kernel(x_prompt, x_sample, state_conv, cache_k, cache_v, page_table, a_norm_g, a_w_in, a_conv_w, a_conv_b,
           a_ln_g, a_ln_b, a_w_out, kv_norm_g, w_kv, b_norm_g, b_w_in, b_w_out, final_norm_g):
    n_seq, seq, _ = x_prompt.shape
    n_dec, dec_seq, _ = x_sample.shape
    assert a_w_in.shape[0] == 1 and b_w_in.shape[0] == 1, "one conv layer and one attention layer"
    assert seq % MOBA_BLOCK == 0 and seq >= HIST

    vec = lambda a: a.reshape(1, D_MODEL)
    a_g, cb, ln_g, ln_b = vec(a_norm_g[0]), vec(a_conv_b[0]), vec(a_ln_g[0]), vec(a_ln_b[0])
    kv_g, b_g, fin_g = vec(kv_norm_g), vec(b_norm_g[0]), vec(final_norm_g)
    cw = a_conv_w[0]
    w_in_bf, w_out_bf = a_w_in[0].astype(BF16), a_w_out[0].astype(BF16)
    w_kv_bf, w_kvt_bf = w_kv.astype(BF16), w_kv.T.astype(BF16)
    bw_in_bf, bw_out_bf = b_w_in[0].astype(BF16), b_w_out[0].astype(BF16)
    slopes = jnp.exp2(-8.0 * jnp.arange(1, N_HEADS + 1, dtype=F32) / N_HEADS)

    xp = x_prompt.reshape(n_seq * seq, D_MODEL)
    glu_p, sz_p = _inproj(xp, a_g, w_in_bf, _ROWS_PER_STEP)
    glu_p3 = glu_p.reshape(n_seq, seq, D_MODEL)
    x1_p, q_p, sz2_p, kt_p, vt_p, ktb_p, vtb_p, kmt_p = _mid_prompt(
        glu_p3, sz_p.reshape(n_seq, seq, D_MODEL), x_prompt,
        cw, cb, ln_g, ln_b, w_out_bf, kv_g, w_kvt_bf, b_g, bw_in_bf)
    bias_p = _select(q_p, kmt_p)
    o_p = _attend_prompt(q_p, bias_p, ktb_p, vtb_p, slopes)
    y_p = _out(o_p.reshape(n_seq * seq, D_MODEL), sz2_p.reshape(n_seq * seq, D_MODEL),
               x1_p.reshape(n_seq * seq, D_MODEL), bw_out_bf, fin_g, _ROWS_PER_STEP)

    n_rows = n_dec * dec_seq
    xs = x_sample.transpose(1, 0, 2).reshape(n_rows, D_MODEL)
    glu_s, sz_s = _inproj(xs, a_g, w_in_bf, _ROWS_PER_STEP)
    glu_s3 = glu_s.reshape(dec_seq, n_dec, D_MODEL)
    hist_t = state_conv[0].transpose(1, 0, 2)
    y_s = _conv_sample(hist_t, glu_s3, cw, cb)
    x1_s, q_s, sz2_s, k_s, v_s, kt_s, vt_s = _post_sample(
        y_s.reshape(n_rows, D_MODEL), sz_s, xs, ln_g, ln_b, w_out_bf, kv_g, w_kv_bf, w_kvt_bf, b_g, bw_in_bf,
        n_dec, _ROWS_PER_STEP)
    by_seq = lambda a: a.reshape(dec_seq, n_dec, D_MODEL).transpose(1, 0, 2)
    slope_rows = jnp.repeat(slopes, dec_seq).reshape(N_HEADS * dec_seq, 1)
    pages = cache_k.shape[0]
    cache_kt = cache_k.transpose(0, 2, 3, 1).reshape(pages, D_MODEL, PAGE_SIZE)
    cache_vt = cache_v.transpose(0, 2, 3, 1).reshape(pages, D_MODEL, PAGE_SIZE)
    o_s = _attend_sample(by_seq(q_s), by_seq(k_s), by_seq(v_s), cache_kt, cache_vt, page_table, slope_rows)
    y_s = _out(o_s.transpose(1, 0, 2).reshape(n_rows, D_MODEL), sz2_s, x1_s, bw_out_bf, fin_g, _ROWS_PER_STEP)

    hist_new = jnp.concatenate([hist_t[dec_seq:], glu_s3], axis=0)[-HIST:]
    heads_p = lambda a: a.reshape(n_seq, N_HEADS, HEAD_DIM, seq).transpose(0, 3, 1, 2)
    heads_s = lambda a: a.reshape(dec_seq, N_HEADS, HEAD_DIM, n_dec).transpose(3, 0, 1, 2)
    return (y_p.reshape(n_seq, seq, D_MODEL), by_seq(y_s),
            glu_p3[:, seq - HIST:][None], hist_new.transpose(1, 0, 2)[None],
            heads_p(kt_p), heads_p(vt_p), heads_s(kt_s), heads_s(vt_s))
```

```python
import functools
import math

import jax
import jax.numpy as jnp
from jax import lax
from jax.experimental import pallas as pl
from jax.experimental.pallas import tpu as pltpu

D_MODEL = 1024
N_HEADS = 16
HEAD_DIM = 64
CONV_WIDTH = 31
HIST = CONV_WIDTH - 1
MOBA_BLOCK = 256
MOBA_TOPK = 3
PAGE_SIZE = 128
NORM_EPS = 1e-6
MASK_VALUE = -1e30

LANES = 128
SUBLANES = 8
BF16_ROWS = 16
VMEM_LIMIT = 56 * 1024 * 1024
LOG2E = math.log2(math.e)

F32 = jnp.float32
BF16 = jnp.bfloat16


def _idiv(x, n):
    assert n & (n - 1) == 0
    return lax.shift_right_logical(x, n.bit_length() - 1)


def _imod(x, n):
    assert n & (n - 1) == 0
    return lax.bitwise_and(x, n - 1)


def _sigmoid(x):
    return 1.0 / (1.0 + jnp.exp(-x))


def _dot(a, b, precision=None):
    return jnp.dot(a, b, preferred_element_type=F32, precision=precision)


def _dot_nt(a, b):
    return lax.dot_general(a, b, (((1,), (1,)), ((), ())), preferred_element_type=F32)


def _params(*semantics):
    return pltpu.CompilerParams(dimension_semantics=semantics,
                                vmem_limit_bytes=VMEM_LIMIT)


def _const_spec(shape):
    zeros = (0,) * len(shape)
    return pl.BlockSpec(shape, lambda *_: zeros, pipeline_mode=pl.Buffered(1))


def _rank_is_top(scores, valid, n_top):
    n = scores.shape[0]
    row = lax.broadcasted_iota(jnp.int32, scores.shape, 0)
    s = jnp.where(valid, scores, MASK_VALUE)
    rank = jnp.zeros(scores.shape, jnp.int32)
    for j in range(n):
        sj = s[j:j + 1, :]
        rank = rank + jnp.where(row > j, jnp.where(sj >= s, 1, 0), jnp.where(sj > s, 1, 0))
    return jnp.where(valid, rank, n_top) < n_top


def _inproj_body(x_ref, g_ref, w_ref, glu_ref, sz_ref):
    x = x_ref[...]
    h = x * lax.rsqrt(jnp.mean(x * x, axis=-1, keepdims=True) + NORM_EPS) * g_ref[...]
    hb = h.astype(BF16)
    a = _dot(hb, w_ref[:, 0:D_MODEL])
    b = _dot(hb, w_ref[:, D_MODEL:2 * D_MODEL])
    z = _dot(hb, w_ref[:, 2 * D_MODEL:3 * D_MODEL])
    glu_ref[...] = a * _sigmoid(b)
    sz_ref[...] = z * _sigmoid(z)


def _inproj(x, norm_g, w_in_bf, rows_per_step):
    n_rows = x.shape[0]
    row_spec = pl.BlockSpec((rows_per_step, D_MODEL), lambda i: (i, 0))
    return pl.pallas_call(
        _inproj_body,
        grid=(n_rows // rows_per_step,),
        in_specs=[row_spec, _const_spec((1, D_MODEL)), _const_spec((D_MODEL, 3 * D_MODEL))],
        out_specs=[row_spec, row_spec],
        out_shape=[jax.ShapeDtypeStruct((n_rows, D_MODEL), F32)] * 2,
        compiler_params=_params("arbitrary"),
        name="inproj",
    )(x, norm_g, w_in_bf)


def _post_conv(y, sz, x, ln_g, ln_b, w_out_ref, kv_g, b_g, bw_z_ref):
    mu = jnp.mean(y, axis=-1, keepdims=True)
    yc = y - mu
    var = jnp.mean(yc * yc, axis=-1, keepdims=True)
    ln = yc * lax.rsqrt(var + NORM_EPS) * ln_g + ln_b
    u = ln * _sigmoid(ln) * sz
    x1 = x + _dot(u.astype(BF16), w_out_ref[...])
    xn = x1 * lax.rsqrt(jnp.mean(x1 * x1, axis=-1, keepdims=True) + NORM_EPS)
    xb = (xn * b_g).astype(BF16)
    z2 = _dot(xb, bw_z_ref[...])
    return x1, z2 * _sigmoid(z2), (xn * kv_g).astype(BF16), xb


_MID_ROWS = MOBA_BLOCK
_CARRY = 32
_CONV_ROWS = 128
_TAP0 = _CARRY - HIST
_SHIFT_ROWS = _CARRY + _MID_ROWS - SUBLANES


def _mid_prompt_body(glu_ref, sz_ref, x_ref, cw_ref, cb_ref, lng_ref, lnb_ref, wout_ref,
                     kvg_ref, wk_ref, wkvt_ref, bg_ref, bwz_ref, bwqt_ref,
                     x1_ref, sz2_ref, qt_ref, kt_ref, vt_ref, kb_ref, vtb_ref, kmt_ref,
                     full_ref, shift_ref, y_ref):
    t = pl.program_id(1)

    @pl.when(t == 0)
    def _():
        full_ref[0:_CARRY, :] = jnp.zeros((_CARRY, D_MODEL), F32)

    @pl.when(t > 0)
    def _():
        full_ref[0:_CARRY, :] = full_ref[_MID_ROWS:_MID_ROWS + _CARRY, :]

    full_ref[_CARRY:_CARRY + _MID_ROWS, :] = glu_ref[0]

    def lane_group(c, carry):
        off = pl.multiple_of(c * LANES, LANES)
        for s in range(1, SUBLANES):
            shift_ref[s - 1] = full_ref[pl.ds(s, _SHIFT_ROWS), pl.ds(off, LANES)]
        for rc in range(_MID_ROWS // _CONV_ROWS):
            acc = jnp.zeros((_CONV_ROWS, LANES), F32)
            for w in range(CONV_WIDTH):
                r0 = _TAP0 + w + rc * _CONV_ROWS
                s = r0 % SUBLANES
                if s == 0:
                    rows = full_ref[pl.ds(r0, _CONV_ROWS), pl.ds(off, LANES)]
                else:
                    rows = shift_ref[s - 1, pl.ds(r0 - s, _CONV_ROWS), :]
                acc = acc + rows * cw_ref[w:w + 1, pl.ds(off, LANES)]
            y_ref[pl.ds(rc * _CONV_ROWS, _CONV_ROWS), pl.ds(off, LANES)] = acc + cb_ref[:, pl.ds(off, LANES)]
        return carry

    lax.fori_loop(0, D_MODEL // LANES, lane_group, 0)

    x1, sz2, xkv, xb = _post_conv(y_ref[...], sz_ref[0], x_ref[0], lng_ref[...], lnb_ref[...], wout_ref,
                                  kvg_ref[...], bg_ref[...], bwz_ref)
    x1_ref[0] = x1
    sz2_ref[0] = sz2
    qt_ref[0] = _dot_nt(bwqt_ref[...], xb)
    kvt = _dot_nt(wkvt_ref[...], xkv)
    kt = kvt[:D_MODEL]
    vt = kvt[D_MODEL:]
    kt_ref[0] = kt
    vt_ref[0] = vt
    vtb_ref[0] = vt.astype(BF16)
    kb_ref[0] = _dot(xkv, wk_ref[...]).astype(BF16)
    mean_col = jnp.sum(kt, axis=1, keepdims=True) * (1.0 / MOBA_BLOCK)
    lane = lax.broadcasted_iota(jnp.int32, (D_MODEL, LANES), 1)

    @pl.when(t == 0)
    def _():
        kmt_ref[0] = jnp.where(lane == 0, mean_col, 0.0)

    @pl.when(t > 0)
    def _():
        kmt_ref[0] = jnp.where(lane == t, mean_col, kmt_ref[0])


def _mid_prompt(glu, sz, x, cw, cb, ln_g, ln_b, w_out_bf, kv_g, w_k_bf, w_kvt_bf, b_g, bw_z_bf, bw_qt_bf):
    n_seq, seq, _ = x.shape
    n_tiles = seq // _MID_ROWS
    assert n_tiles <= LANES
    tile = pl.BlockSpec((1, _MID_ROWS, D_MODEL), lambda b, t: (b, t, 0))
    tile_t = pl.BlockSpec((1, D_MODEL, _MID_ROWS), lambda b, t: (b, 0, t))
    vec = _const_spec((1, D_MODEL))
    square = _const_spec((D_MODEL, D_MODEL))
    f32_rows = jax.ShapeDtypeStruct((n_seq, seq, D_MODEL), F32)
    f32_t = jax.ShapeDtypeStruct((n_seq, D_MODEL, seq), F32)
    return pl.pallas_call(
        _mid_prompt_body,
        grid=(n_seq, n_tiles),
        in_specs=[tile, tile, tile, _const_spec((CONV_WIDTH, D_MODEL)), vec, vec, vec,
                  square, vec, square, _const_spec((2 * D_MODEL, D_MODEL)), vec, square, square],
        out_specs=[tile, tile, tile_t, tile_t, tile_t, tile, tile_t,
                   pl.BlockSpec((1, D_MODEL, LANES), lambda b, t: (b, 0, 0))],
        out_shape=[f32_rows, f32_rows, f32_t, f32_t, f32_t,
                   jax.ShapeDtypeStruct((n_seq, seq, D_MODEL), BF16),
                   jax.ShapeDtypeStruct((n_seq, D_MODEL, seq), BF16),
                   jax.ShapeDtypeStruct((n_seq, D_MODEL, LANES), F32)],
        scratch_shapes=[pltpu.VMEM((_CARRY + _MID_ROWS, D_MODEL), F32),
                        pltpu.VMEM((SUBLANES - 1, _SHIFT_ROWS, LANES), F32),
                        pltpu.VMEM((_MID_ROWS, D_MODEL), F32)],
        compiler_params=_params("arbitrary", "arbitrary"),
        name="mid_prompt",
    )(glu, sz, x, cw, cb, ln_g, ln_b, w_out_bf, kv_g, w_k_bf, w_kvt_bf, b_g, bw_z_bf, bw_qt_bf)


def _conv_sample_body(dec_seq, hist_ref, glu_ref, cw_ref, cb_ref, y_ref):
    def tap(r):
        return hist_ref[r] if r < HIST else glu_ref[r - HIST]

    for t in range(dec_seq):
        acc = tap(t) * cw_ref[0:1, :]
        for w in range(1, CONV_WIDTH):
            acc = acc + tap(t + w) * cw_ref[w:w + 1, :]
        y_ref[t] = acc + cb_ref[...]


def _conv_sample(hist_t, glu_t, cw, cb):
    dec_seq, n_seq, _ = glu_t.shape
    lane_blk = lambda rows: pl.BlockSpec((rows, n_seq, LANES), lambda c: (0, 0, c))
    return pl.pallas_call(
        functools.partial(_conv_sample_body, dec_seq),
        grid=(D_MODEL // LANES,),
        in_specs=[lane_blk(HIST), lane_blk(dec_seq),
                  pl.BlockSpec((CONV_WIDTH, LANES), lambda c: (0, c)),
                  pl.BlockSpec((1, LANES), lambda c: (0, c))],
        out_specs=lane_blk(dec_seq),
        out_shape=jax.ShapeDtypeStruct((dec_seq, n_seq, D_MODEL), F32),
        compiler_params=_params("arbitrary"),
        name="conv_sample",
    )(hist_t, glu_t, cw, cb)


def _post_sample_body(t_per_step, n_seq, y_ref, sz_ref, x_ref, lng_ref, lnb_ref, wout_ref,
                      kvg_ref, wkv_ref, wkvt_ref, bg_ref, bwz_ref, bwq_ref,
                      x1_ref, q_ref, sz2_ref, k_ref, v_ref, kt_ref, vt_ref):
    x1, sz2, xkv, xb = _post_conv(y_ref[...], sz_ref[...], x_ref[...], lng_ref[...], lnb_ref[...], wout_ref,
                                  kvg_ref[...], bg_ref[...], bwz_ref)
    x1_ref[...] = x1
    sz2_ref[...] = sz2
    q_ref[...] = _dot(xb, bwq_ref[...])
    kv = _dot(xkv, wkv_ref[...])
    k_ref[...] = kv[:, :D_MODEL]
    v_ref[...] = kv[:, D_MODEL:]
    kvt = _dot_nt(wkvt_ref[...], xkv)
    for i in range(t_per_step):
        kt_ref[i] = kvt[:D_MODEL, i * n_seq:(i + 1) * n_seq]
        vt_ref[i] = kvt[D_MODEL:, i * n_seq:(i + 1) * n_seq]


def _post_sample(y, sz, x, ln_g, ln_b, w_out_bf, kv_g, w_kv_bf, w_kvt_bf, b_g, bw_z_bf, bw_q_bf,
                 n_seq, rows_per_step):
    n_rows = y.shape[0]
    dec_seq = n_rows // n_seq
    t_per_step = rows_per_step // n_seq
    assert n_seq % LANES == 0 and rows_per_step % n_seq == 0
    row_spec = pl.BlockSpec((rows_per_step, D_MODEL), lambda i: (i, 0))
    t_spec = pl.BlockSpec((t_per_step, D_MODEL, n_seq), lambda i: (i, 0, 0))
    vec = _const_spec((1, D_MODEL))
    square = _const_spec((D_MODEL, D_MODEL))
    f32_rows = jax.ShapeDtypeStruct((n_rows, D_MODEL), F32)
    f32_t = jax.ShapeDtypeStruct((dec_seq, D_MODEL, n_seq), F32)
    return pl.pallas_call(
        functools.partial(_post_sample_body, t_per_step, n_seq),
        grid=(n_rows // rows_per_step,),
        in_specs=[row_spec, row_spec, row_spec, vec, vec, square, vec,
                  _const_spec((D_MODEL, 2 * D_MODEL)), _const_spec((2 * D_MODEL, D_MODEL)),
                  vec, square, square],
        out_specs=[row_spec] * 5 + [t_spec] * 2,
        out_shape=[f32_rows] * 5 + [f32_t] * 2,
        compiler_params=_params("arbitrary"),
        name="post_sample",
    )(y, sz, x, ln_g, ln_b, w_out_bf, kv_g, w_kv_bf, w_kvt_bf, b_g, bw_z_bf, bw_q_bf)


_HEADS_PER_STEP = 8
_PV_ROWS = HEAD_DIM + BF16_ROWS


def _softmax_probs(t, shift, m_ref, h, first):
    m_cur = jnp.max(t, axis=0, keepdims=True) - shift
    if first:
        m_new, alpha = m_cur, None
    else:
        m_prev = m_ref[h, 0:1, :]
        m_new = jnp.maximum(m_prev, m_cur)
        alpha = jnp.exp2(m_prev - m_new)
    m_ref[h] = jnp.broadcast_to(m_new, (SUBLANES, t.shape[1]))
    return jnp.exp2(t - (m_new + shift)).astype(BF16), alpha


def _attend_prompt_body(n_blocks, slopes_ref, qt_ref, km_ref, k_ref, vt_ref, o_ref,
                        qaug_ref, sd_ref, m_ref, acc_ref):
    g = pl.program_id(1)
    i = pl.program_id(2)
    heads = range(_HEADS_PER_STEP)
    key = lax.broadcasted_iota(jnp.int32, (MOBA_BLOCK, MOBA_BLOCK), 0)
    qry = lax.broadcasted_iota(jnp.int32, (MOBA_BLOCK, MOBA_BLOCK), 1)
    dist = (qry - key).astype(F32)
    blk = lax.broadcasted_iota(jnp.int32, (n_blocks, MOBA_BLOCK), 0)
    feat_head = _idiv(lax.broadcasted_iota(jnp.int32, (LANES, MOBA_BLOCK), 0), HEAD_DIM)
    lane_head = _idiv(lax.broadcasted_iota(jnp.int32, (n_blocks, LANES), 1), HEAD_DIM)
    own0 = pl.multiple_of(i * MOBA_BLOCK, MOBA_BLOCK)
    ones_rows = jnp.ones((BF16_ROWS, MOBA_BLOCK), BF16)
    bias_pad = jnp.zeros((LANES - n_blocks, MOBA_BLOCK), F32)
    slope2 = [slopes_ref[g * _HEADS_PER_STEP + h] * LOG2E for h in heads]

    def pair_rows(h):
        return slice((h // 2) * LANES, (h // 2 + 1) * LANES)

    def v_aug(h, off):
        return jnp.concatenate([vt_ref[0, h * HEAD_DIM:(h + 1) * HEAD_DIM, pl.ds(off, MOBA_BLOCK)], ones_rows], axis=0)

    scores = [_dot(jnp.where(lane_head == h % 2, km_ref[0, :, pair_rows(h)], 0.0), qt_ref[0, pair_rows(h), :],
                   precision=lax.Precision.HIGHEST) for h in heads]
    q_own = []
    for h in heads:
        chosen = _rank_is_top(scores[h], blk < i, MOBA_TOPK)
        bias = jnp.where(chosen, 0.0, MASK_VALUE)
        q_h = jnp.where(feat_head == h % 2, qt_ref[0, pair_rows(h), :] * (HEAD_DIM ** -0.5 * LOG2E), 0.0).astype(BF16)
        qaug_ref[h] = jnp.concatenate([q_h, jnp.concatenate([bias, bias_pad], axis=0).astype(BF16)], axis=0)
        sd_ref[h] = slope2[h] * dist
        q_own.append(q_h)

    s_own = [_dot(k_ref[0, pl.ds(own0, MOBA_BLOCK), pair_rows(h)], q_own[h]) for h in heads]
    p_own = [_softmax_probs(jnp.where(qry >= key, s_own[h] - sd_ref[h], MASK_VALUE), 0.0, m_ref, h, first=True)[0]
             for h in heads]
    for h in heads:
        acc_ref[h] = _dot(v_aug(h, own0), p_own[h])

    lane = lax.broadcasted_iota(jnp.int32, (MOBA_BLOCK, LANES), 1)

    def past_block(j, carry):
        off = pl.multiple_of(j * MOBA_BLOCK, MOBA_BLOCK)
        marker = jnp.where(lane == j, 1.0, 0.0).astype(BF16)
        gap = ((i - j) * MOBA_BLOCK).astype(F32)
        s = [_dot(jnp.concatenate([k_ref[0, pl.ds(off, MOBA_BLOCK), pair_rows(h)], marker], axis=1), qaug_ref[h])
             for h in heads]
        probs = [_softmax_probs(s[h] - sd_ref[h], slope2[h] * gap, m_ref, h, first=False) for h in heads]
        for h in heads:
            p, alpha = probs[h]
            acc_ref[h] = alpha * acc_ref[h] + _dot(v_aug(h, off), p)
        return carry

    lax.fori_loop(0, i, past_block, 0)

    for pair in range(_HEADS_PER_STEP // 2):
        outs = []
        for e in range(2):
            acc = acc_ref[2 * pair + e]
            outs.append(acc[:HEAD_DIM] / acc[HEAD_DIM:HEAD_DIM + 1])
        o_ref[0, :, pair * LANES:(pair + 1) * LANES] = jnp.concatenate(outs, axis=0).T


def _attend_prompt(qt, kmean, k_bf, vt_bf, slopes):
    n_seq, _, seq = qt.shape
    n_blocks = seq // MOBA_BLOCK
    width = _HEADS_PER_STEP * HEAD_DIM
    assert n_blocks <= LANES and n_blocks % SUBLANES == 0 and 2 * HEAD_DIM == LANES
    return pl.pallas_call(
        functools.partial(_attend_prompt_body, n_blocks),
        grid=(n_seq, D_MODEL // width, n_blocks),
        in_specs=[pl.BlockSpec(memory_space=pltpu.SMEM),
                  pl.BlockSpec((1, width, MOBA_BLOCK), lambda b, g, i: (b, g, i)),
                  pl.BlockSpec((1, n_blocks, width), lambda b, g, i: (b, 0, g)),
                  pl.BlockSpec((1, seq, width), lambda b, g, i: (b, 0, g)),
                  pl.BlockSpec((1, width, seq), lambda b, g, i: (b, g, 0))],
        out_specs=pl.BlockSpec((1, MOBA_BLOCK, width), lambda b, g, i: (b, i, g)),
        out_shape=jax.ShapeDtypeStruct((n_seq, seq, D_MODEL), F32),
        scratch_shapes=[pltpu.VMEM((_HEADS_PER_STEP, 2 * LANES, MOBA_BLOCK), BF16),
                        pltpu.VMEM((_HEADS_PER_STEP, MOBA_BLOCK, MOBA_BLOCK), F32),
                        pltpu.VMEM((_HEADS_PER_STEP, SUBLANES, MOBA_BLOCK), F32),
                        pltpu.VMEM((_HEADS_PER_STEP, _PV_ROWS, MOBA_BLOCK), F32)],
        compiler_params=_params("arbitrary", "arbitrary", "arbitrary"),
        name="moba_attend_prompt",
    )(slopes, qt, kmean, k_bf, vt_bf)


_BLOCKS_PER_STEP = 4
_PAGES_PER_BLOCK = MOBA_BLOCK // PAGE_SIZE


def _attend_sample_body(dec_seq, n_past, past_len, pt_ref, q_ref, slope_ref, *refs):
    n_pg = _BLOCKS_PER_STEP * _PAGES_PER_BLOCK
    k_pages, v_pages = refs[:n_pg], refs[n_pg:2 * n_pg]
    kn_ref, vn_ref, o_ref, qdt_ref, qdb_ref, sc_ref, m_ref, l_ref, acc_ref = refs[2 * n_pg:]
    jj = pl.program_id(1)
    rows = N_HEADS * dec_seq
    lane = lax.broadcasted_iota(jnp.int32, (rows, LANES), 1)
    slope = slope_ref[...]

    @pl.when(jj == 0)
    def _():
        q = q_ref[0]
        q_rows = jnp.concatenate([q] * N_HEADS, axis=0)
        r_head = _idiv(lax.broadcasted_iota(jnp.int32, (rows, D_MODEL), 0), dec_seq)
        l_head = _idiv(lax.broadcasted_iota(jnp.int32, (rows, D_MODEL), 1), HEAD_DIM)
        q_diag = jnp.where(r_head == l_head, q_rows, 0.0)
        qdt_ref[...] = q_diag.T
        qdb_ref[...] = (q_diag * (HEAD_DIM ** -0.5)).astype(BF16)
        m_ref[...] = jnp.zeros((rows, LANES), F32)
        l_ref[...] = jnp.zeros((rows, LANES), F32)

    q_off = _imod(lax.broadcasted_iota(jnp.int32, (rows, MOBA_BLOCK), 0), dec_seq)
    k_off = lax.broadcasted_iota(jnp.int32, (rows, MOBA_BLOCK), 1)
    blocks = range(_BLOCKS_PER_STEP)
    js = [jj * _BLOCKS_PER_STEP + b for b in blocks]

    def block_of(page_refs, b):
        return jnp.concatenate([r[0] for r in page_refs[b * _PAGES_PER_BLOCK:(b + 1) * _PAGES_PER_BLOCK]], axis=1)

    s = []
    for b in blocks:
        kt_blk = block_of(k_pages, b)
        s.append(_dot(qdb_ref[...], kt_blk.astype(BF16)))
        k_mean = jnp.sum(kt_blk, axis=1, keepdims=True) * (1.0 / MOBA_BLOCK)
        sc_ref[pl.ds(js[b], 1), :] = jnp.sum(qdt_ref[...] * k_mean, axis=0, keepdims=True)
    p = []
    m_all, l_all = m_ref[...], l_ref[...]
    for b in blocks:
        t = s[b] - slope * (q_off - k_off + (past_len - js[b] * MOBA_BLOCK)).astype(F32)
        m_j = jnp.max(t, axis=1, keepdims=True)
        e = jnp.exp(t - m_j)
        m_all = jnp.where(lane == js[b], m_j, m_all)
        l_all = jnp.where(lane == js[b], jnp.sum(e, axis=1, keepdims=True), l_all)
        p.append(e.astype(BF16))
    m_ref[...] = m_all
    l_ref[...] = l_all
    for b in blocks:
        acc_ref[js[b]] = _dot_nt(p[b], block_of(v_pages, b).astype(BF16))

    @pl.when(jj == n_past // _BLOCKS_PER_STEP - 1)
    def _():
        pad = jnp.zeros((LANES - dec_seq, D_MODEL), F32)
        k_new = jnp.concatenate([kn_ref[0], pad], axis=0).astype(BF16)
        v_new = jnp.concatenate([vn_ref[0], pad], axis=0).astype(BF16)
        s_own = _dot_nt(qdb_ref[...], k_new)
        d_own = _imod(lax.broadcasted_iota(jnp.int32, (rows, LANES), 0), dec_seq) - lane
        t_own = jnp.where(d_own >= 0, s_own - slope * d_own.astype(F32), MASK_VALUE)
        m_own = jnp.max(t_own, axis=1, keepdims=True)
        p_own = jnp.exp(t_own - m_own)
        l_own = jnp.sum(p_own, axis=1, keepdims=True)
        acc_own = _dot(p_own.astype(BF16), v_new)

        always = lax.broadcasted_iota(jnp.int32, (n_past, rows), 0) >= 0
        chosen_t = jnp.where(_rank_is_top(sc_ref[...], always, min(MOBA_TOPK, n_past)), 1.0, 0.0)
        chosen = jnp.concatenate([chosen_t, jnp.zeros((LANES - n_past, rows), F32)], axis=0).T > 0.5

        m_all = m_ref[...]
        m_top = jnp.maximum(jnp.max(jnp.where(chosen, m_all, MASK_VALUE), axis=1, keepdims=True), m_own)
        w_all = jnp.where(chosen, jnp.exp(m_all - m_top), 0.0)
        w_own = jnp.exp(m_own - m_top)
        denom = jnp.sum(w_all * l_ref[...], axis=1, keepdims=True) + w_own * l_own
        num = w_own * acc_own
        for b in range(n_past):
            num = num + w_all[:, b:b + 1] * acc_ref[b]
        out = num / denom
        l_head = _idiv(lax.broadcasted_iota(jnp.int32, (dec_seq, D_MODEL), 1), HEAD_DIM)
        res = jnp.zeros((dec_seq, D_MODEL), F32)
        for h in range(N_HEADS):
            res = jnp.where(l_head == h, out[h * dec_seq:(h + 1) * dec_seq, :], res)
        o_ref[0] = res


def _attend_sample(q, k_new, v_new, cache_kt, cache_vt, page_table, slope_rows):
    n_seq, dec_seq, _ = q.shape
    n_pages = page_table.shape[1]
    pages_per_step = _BLOCKS_PER_STEP * _PAGES_PER_BLOCK
    assert n_pages % pages_per_step == 0 and dec_seq % SUBLANES == 0
    n_past = n_pages // _PAGES_PER_BLOCK
    past_len = n_pages * PAGE_SIZE
    rows = N_HEADS * dec_seq
    assert n_past % SUBLANES == 0 and n_past <= LANES and dec_seq <= LANES and rows % LANES == 0
    new_spec = pl.BlockSpec((1, dec_seq, D_MODEL), lambda b, jj, pt: (b, 0, 0))

    def page_spec(which):
        return pl.BlockSpec((1, D_MODEL, PAGE_SIZE), lambda b, jj, pt: (pt[b, pages_per_step * jj + which], 0, 0))

    page_specs = [page_spec(w) for w in range(pages_per_step)]
    stats = pltpu.VMEM((rows, LANES), F32)
    grid_spec = pltpu.PrefetchScalarGridSpec(
        num_scalar_prefetch=1,
        grid=(n_seq, n_pages // pages_per_step),
        in_specs=[new_spec, pl.BlockSpec((rows, 1), lambda b, jj, pt: (0, 0))] + page_specs + page_specs
                 + [new_spec, new_spec],
        out_specs=new_spec,
        scratch_shapes=[pltpu.VMEM((D_MODEL, rows), F32), pltpu.VMEM((rows, D_MODEL), BF16),
                        pltpu.VMEM((n_past, rows), F32), stats, stats,
                        pltpu.VMEM((n_past, rows, D_MODEL), F32)],
    )
    return pl.pallas_call(
        functools.partial(_attend_sample_body, dec_seq, n_past, past_len),
        grid_spec=grid_spec,
        out_shape=jax.ShapeDtypeStruct((n_seq, dec_seq, D_MODEL), F32),
        compiler_params=_params("arbitrary", "arbitrary"),
        name="moba_attend_sample",
    )(page_table, q, slope_rows, *([cache_kt] * pages_per_step), *([cache_vt] * pages_per_step), k_new, v_new)


def _out_body(o_ref, sz2_ref, x1_ref, w_ref, g_ref, y_ref):
    x2 = x1_ref[...] + _dot((o_ref[...] * sz2_ref[...]).astype(BF16), w_ref[...])
    y_ref[...] = x2 * lax.rsqrt(jnp.mean(x2 * x2, axis=-1, keepdims=True) + NORM_EPS) * g_ref[...]


def _out(o, sz2, x1, w_out_bf, final_g, rows_per_step):
    n_rows = o.shape[0]
    row_spec = pl.BlockSpec((rows_per_step, D_MODEL), lambda i: (i, 0))
    return pl.pallas_call(
        _out_body,
        grid=(n_rows // rows_per_step,),
        in_specs=[row_spec, row_spec, row_spec, _const_spec((D_MODEL, D_MODEL)), _const_spec((1, D_MODEL))],
        out_specs=row_spec,
        out_shape=jax.ShapeDtypeStruct((n_rows, D_MODEL), F32),
        compiler_params=_params("arbitrary"),
        name="out",
    )(o, sz2, x1, w_out_bf, final_g)


_ROWS_PER_STEP = 256


def kernel(x_prompt, x_sample, state_conv, cache_k, cache_v, page_table, a_norm_g, a_w_in, a_conv_w, a_conv_b,
           a_ln_g, a_ln_b, a_w_out, kv_norm_g, w_kv, b_norm_g, b_w_in, b_w_out, final_norm_g):
    n_seq, seq, _ = x_prompt.shape
    n_dec, dec_seq, _ = x_sample.shape
    assert a_w_in.shape[0] == 1 and b_w_in.shape[0] == 1, "one conv layer and one attention layer"
    assert seq % MOBA_BLOCK == 0 and seq >= HIST

    vec = lambda a: a.reshape(1, D_MODEL)
    a_g, cb, ln_g, ln_b = vec(a_norm_g[0]), vec(a_conv_b[0]), vec(a_ln_g[0]), vec(a_ln_b[0])
    kv_g, b_g, fin_g = vec(kv_norm_g), vec(b_norm_g[0]), vec(final_norm_g)
    cw = a_conv_w[0]
    w_in_bf, w_out_bf = a_w_in[0].astype(BF16), a_w_out[0].astype(BF16)
    w_kv_bf, w_kvt_bf = w_kv.astype(BF16), w_kv.T.astype(BF16)
    bw_q_bf, bw_z_bf = b_w_in[0, :, :D_MODEL].astype(BF16), b_w_in[0, :, D_MODEL:].astype(BF16)
    bw_out_bf = b_w_out[0].astype(BF16)
    slopes = jnp.exp2(-8.0 * jnp.arange(1, N_HEADS + 1, dtype=F32) / N_HEADS)

    xp = x_prompt.reshape(n_seq * seq, D_MODEL)
    glu_p, sz_p = _inproj(xp, a_g, w_in_bf, _ROWS_PER_STEP)
    glu_p3 = glu_p.reshape(n_seq, seq, D_MODEL)
    x1_p, sz2_p, qt_p, kt_p, vt_p, kb_p, vtb_p, kmt_p = _mid_prompt(
        glu_p3, sz_p.reshape(n_seq, seq, D_MODEL), x_prompt,
        cw, cb, ln_g, ln_b, w_out_bf, kv_g, w_kv_bf[:, :D_MODEL], w_kvt_bf, b_g, bw_z_bf, bw_q_bf.T)
    kmean_p = kmt_p.transpose(0, 2, 1)[:, :seq // MOBA_BLOCK]
    o_p = _attend_prompt(qt_p, kmean_p, kb_p, vtb_p, slopes)
    y_p = _out(o_p.reshape(n_seq * seq, D_MODEL), sz2_p.reshape(n_seq * seq, D_MODEL),
               x1_p.reshape(n_seq * seq, D_MODEL), bw_out_bf, fin_g, _ROWS_PER_STEP)

    n_rows = n_dec * dec_seq
    xs = x_sample.transpose(1, 0, 2).reshape(n_rows, D_MODEL)
    glu_s, sz_s = _inproj(xs, a_g, w_in_bf, _ROWS_PER_STEP)
    glu_s3 = glu_s.reshape(dec_seq, n_dec, D_MODEL)
    hist_t = state_conv[0].transpose(1, 0, 2)
    y_s = _conv_sample(hist_t, glu_s3, cw, cb)
    x1_s, q_s, sz2_s, k_s, v_s, kt_s, vt_s = _post_sample(
        y_s.reshape(n_rows, D_MODEL), sz_s, xs, ln_g, ln_b, w_out_bf, kv_g, w_kv_bf, w_kvt_bf, b_g,
        bw_z_bf, bw_q_bf, n_dec, _ROWS_PER_STEP)
    by_seq = lambda a: a.reshape(dec_seq, n_dec, D_MODEL).transpose(1, 0, 2)
    slope_rows = jnp.repeat(slopes, dec_seq).reshape(N_HEADS * dec_seq, 1)
    pages = cache_k.shape[0]
    cache_kt = cache_k.transpose(0, 2, 3, 1).reshape(pages, D_MODEL, PAGE_SIZE)
    cache_vt = cache_v.transpose(0, 2, 3, 1).reshape(pages, D_MODEL, PAGE_SIZE)
    o_s = _attend_sample(by_seq(q_s), by_seq(k_s), by_seq(v_s), cache_kt, cache_vt, page_table, slope_rows)
    y_s = _out(o_s.transpose(1, 0, 2).reshape(n_rows, D_MODEL), sz2_s, x1_s, bw_out_bf, fin_g, _ROWS_PER_STEP)

    hist_new = jnp.concatenate([hist_t[dec_seq:], glu_s3], axis=0)[-HIST:]
    heads_p = lambda a: a.reshape(n_seq, N_HEADS, HEAD_DIM, seq).transpose(0, 3, 1, 2)
    heads_s = lambda a: a.reshape(dec_seq, N_HEADS, HEAD_DIM, n_dec).transpose(3, 0, 1, 2)
    return (y_p.reshape(n_seq, seq, D_MODEL), by_seq(y_s),
            glu_p3[:, seq - HIST:][None], hist_new.transpose(1, 0, 2)[None],
            heads_p(kt_p), heads_p(vt_p), heads_s(kt_s), heads_s(vt_s))
```

```python
import functools
import math

import jax
import jax.numpy as jnp
from jax import lax
from jax.experimental import pallas as pl
from jax.experimental.pallas import tpu as pltpu

D_MODEL = 1024
N_HEADS = 16
HEAD_DIM = 64
CONV_WIDTH = 31
HIST = CONV_WIDTH - 1
MOBA_BLOCK = 256
MOBA_TOPK = 3
PAGE_SIZE = 128
NORM_EPS = 1e-6
MASK_VALUE = -1e30

LANES = 128
SUBLANES = 8
BF16_ROWS = 16
VMEM_LIMIT = 56 * 1024 * 1024
LOG2E = math.log2(math.e)

F32 = jnp.float32
BF16 = jnp.bfloat16


def _idiv(x, n):
    assert n & (n - 1) == 0
    return lax.shift_right_logical(x, n.bit_length() - 1)


def _imod(x, n):
    assert n & (n - 1) == 0
    return lax.bitwise_and(x, n - 1)


def _sigmoid(x):
    return 1.0 / (1.0 + jnp.exp(-x))


def _dot(a, b, precision=None):
    return jnp.dot(a, b, preferred_element_type=F32, precision=precision)


def _dot_nt(a, b):
    return lax.dot_general(a, b, (((1,), (1,)), ((), ())), preferred_element_type=F32)


def _params(*semantics):
    return pltpu.CompilerParams(dimension_semantics=semantics,
                                vmem_limit_bytes=VMEM_LIMIT)


def _const_spec(shape):
    zeros = (0,) * len(shape)
    return pl.BlockSpec(shape, lambda *_: zeros, pipeline_mode=pl.Buffered(1))


def _rank_is_top(scores, valid, n_top):
    n = scores.shape[0]
    row = lax.broadcasted_iota(jnp.int32, scores.shape, 0)
    s = jnp.where(valid, scores, MASK_VALUE)
    rank = jnp.zeros(scores.shape, jnp.int32)
    for j in range(n):
        sj = s[j:j + 1, :]
        rank = rank + jnp.where(row > j, jnp.where(sj >= s, 1, 0), jnp.where(sj > s, 1, 0))
    return jnp.where(valid, rank, n_top) < n_top


def _inproj_body(x_ref, g_ref, w_ref, glu_ref, sz_ref):
    x = x_ref[...]
    h = x * lax.rsqrt(jnp.mean(x * x, axis=-1, keepdims=True) + NORM_EPS) * g_ref[...]
    hb = h.astype(BF16)
    a = _dot(hb, w_ref[:, 0:D_MODEL])
    b = _dot(hb, w_ref[:, D_MODEL:2 * D_MODEL])
    z = _dot(hb, w_ref[:, 2 * D_MODEL:3 * D_MODEL])
    glu_ref[...] = a * _sigmoid(b)
    sz_ref[...] = z * _sigmoid(z)


def _inproj(x, norm_g, w_in_bf, rows_per_step):
    n_rows = x.shape[0]
    row_spec = pl.BlockSpec((rows_per_step, D_MODEL), lambda i: (i, 0))
    return pl.pallas_call(
        _inproj_body,
        grid=(n_rows // rows_per_step,),
        in_specs=[row_spec, _const_spec((1, D_MODEL)), _const_spec((D_MODEL, 3 * D_MODEL))],
        out_specs=[row_spec, row_spec],
        out_shape=[jax.ShapeDtypeStruct((n_rows, D_MODEL), F32)] * 2,
        compiler_params=_params("arbitrary"),
        name="inproj",
    )(x, norm_g, w_in_bf)


def _post_conv(y, sz, x, ln_g, ln_b, w_out_ref, kv_g, b_g, bw_z_ref):
    mu = jnp.mean(y, axis=-1, keepdims=True)
    yc = y - mu
    var = jnp.mean(yc * yc, axis=-1, keepdims=True)
    ln = yc * lax.rsqrt(var + NORM_EPS) * ln_g + ln_b
    u = ln * _sigmoid(ln) * sz
    x1 = x + _dot(u.astype(BF16), w_out_ref[...])
    xn = x1 * lax.rsqrt(jnp.mean(x1 * x1, axis=-1, keepdims=True) + NORM_EPS)
    xb = (xn * b_g).astype(BF16)
    z2 = _dot(xb, bw_z_ref[...])
    return x1, z2 * _sigmoid(z2), (xn * kv_g).astype(BF16), xb


_MID_ROWS = MOBA_BLOCK
_CARRY = 32
_CONV_ROWS = 128
_TAP0 = _CARRY - HIST
_SHIFT_ROWS = _CARRY + _MID_ROWS - SUBLANES


def _mid_prompt_body(glu_ref, sz_ref, x_ref, cw_ref, cb_ref, lng_ref, lnb_ref, wout_ref,
                     kvg_ref, wk_ref, wkvt_ref, bg_ref, bwz_ref, bwqt_ref,
                     x1_ref, sz2_ref, qt_ref, kt_ref, vt_ref, kb_ref, vtb_ref, kmt_ref,
                     full_ref, shift_ref, y_ref):
    t = pl.program_id(1)

    @pl.when(t == 0)
    def _():
        full_ref[0:_CARRY, :] = jnp.zeros((_CARRY, D_MODEL), F32)

    @pl.when(t > 0)
    def _():
        full_ref[0:_CARRY, :] = full_ref[_MID_ROWS:_MID_ROWS + _CARRY, :]

    full_ref[_CARRY:_CARRY + _MID_ROWS, :] = glu_ref[0]

    def lane_group(c, carry):
        off = pl.multiple_of(c * LANES, LANES)
        for s in range(1, SUBLANES):
            shift_ref[s - 1] = full_ref[pl.ds(s, _SHIFT_ROWS), pl.ds(off, LANES)]
        for rc in range(_MID_ROWS // _CONV_ROWS):
            acc = jnp.zeros((_CONV_ROWS, LANES), F32)
            for w in range(CONV_WIDTH):
                r0 = _TAP0 + w + rc * _CONV_ROWS
                s = r0 % SUBLANES
                if s == 0:
                    rows = full_ref[pl.ds(r0, _CONV_ROWS), pl.ds(off, LANES)]
                else:
                    rows = shift_ref[s - 1, pl.ds(r0 - s, _CONV_ROWS), :]
                acc = acc + rows * cw_ref[w:w + 1, pl.ds(off, LANES)]
            y_ref[pl.ds(rc * _CONV_ROWS, _CONV_ROWS), pl.ds(off, LANES)] = acc + cb_ref[:, pl.ds(off, LANES)]
        return carry

    lax.fori_loop(0, D_MODEL // LANES, lane_group, 0)

    x1, sz2, xkv, xb = _post_conv(y_ref[...], sz_ref[0], x_ref[0], lng_ref[...], lnb_ref[...], wout_ref,
                                  kvg_ref[...], bg_ref[...], bwz_ref)
    x1_ref[0] = x1
    sz2_ref[0] = sz2.astype(BF16)
    qt_ref[0] = _dot_nt(bwqt_ref[...], xb)
    kvt = _dot_nt(wkvt_ref[...], xkv)
    kt = kvt[:D_MODEL]
    vt = kvt[D_MODEL:]
    kt_ref[0] = kt
    vt_ref[0] = vt
    vtb_ref[0] = vt.astype(BF16)
    kb_ref[0] = _dot(xkv, wk_ref[...]).astype(BF16)
    mean_col = jnp.sum(kt, axis=1, keepdims=True) * (1.0 / MOBA_BLOCK)
    lane = lax.broadcasted_iota(jnp.int32, (D_MODEL, LANES), 1)

    @pl.when(t == 0)
    def _():
        kmt_ref[0] = jnp.where(lane == 0, mean_col, 0.0)

    @pl.when(t > 0)
    def _():
        kmt_ref[0] = jnp.where(lane == t, mean_col, kmt_ref[0])


def _mid_prompt(glu, sz, x, cw, cb, ln_g, ln_b, w_out_bf, kv_g, w_k_bf, w_kvt_bf, b_g, bw_z_bf, bw_qt_bf):
    n_seq, seq, _ = x.shape
    n_tiles = seq // _MID_ROWS
    assert n_tiles <= LANES
    tile = pl.BlockSpec((1, _MID_ROWS, D_MODEL), lambda b, t: (b, t, 0))
    tile_t = pl.BlockSpec((1, D_MODEL, _MID_ROWS), lambda b, t: (b, 0, t))
    vec = _const_spec((1, D_MODEL))
    square = _const_spec((D_MODEL, D_MODEL))
    f32_rows = jax.ShapeDtypeStruct((n_seq, seq, D_MODEL), F32)
    f32_t = jax.ShapeDtypeStruct((n_seq, D_MODEL, seq), F32)
    return pl.pallas_call(
        _mid_prompt_body,
        grid=(n_seq, n_tiles),
        in_specs=[tile, tile, tile, _const_spec((CONV_WIDTH, D_MODEL)), vec, vec, vec,
                  square, vec, square, _const_spec((2 * D_MODEL, D_MODEL)), vec, square, square],
        out_specs=[tile, tile, tile_t, tile_t, tile_t, tile, tile_t,
                   pl.BlockSpec((1, D_MODEL, LANES), lambda b, t: (b, 0, 0))],
        out_shape=[f32_rows, jax.ShapeDtypeStruct((n_seq, seq, D_MODEL), BF16), f32_t, f32_t, f32_t,
                   jax.ShapeDtypeStruct((n_seq, seq, D_MODEL), BF16),
                   jax.ShapeDtypeStruct((n_seq, D_MODEL, seq), BF16),
                   jax.ShapeDtypeStruct((n_seq, D_MODEL, LANES), F32)],
        scratch_shapes=[pltpu.VMEM((_CARRY + _MID_ROWS, D_MODEL), F32),
                        pltpu.VMEM((SUBLANES - 1, _SHIFT_ROWS, LANES), F32),
                        pltpu.VMEM((_MID_ROWS, D_MODEL), F32)],
        compiler_params=_params("arbitrary", "arbitrary"),
        name="mid_prompt",
    )(glu, sz, x, cw, cb, ln_g, ln_b, w_out_bf, kv_g, w_k_bf, w_kvt_bf, b_g, bw_z_bf, bw_qt_bf)


def _conv_sample_body(dec_seq, hist_ref, glu_ref, cw_ref, cb_ref, y_ref):
    def tap(r):
        return hist_ref[r] if r < HIST else glu_ref[r - HIST]

    for t in range(dec_seq):
        acc = tap(t) * cw_ref[0:1, :]
        for w in range(1, CONV_WIDTH):
            acc = acc + tap(t + w) * cw_ref[w:w + 1, :]
        y_ref[t] = acc + cb_ref[...]


def _conv_sample(hist_t, glu_t, cw, cb):
    dec_seq, n_seq, _ = glu_t.shape
    lane_blk = lambda rows: pl.BlockSpec((rows, n_seq, LANES), lambda c: (0, 0, c))
    return pl.pallas_call(
        functools.partial(_conv_sample_body, dec_seq),
        grid=(D_MODEL // LANES,),
        in_specs=[lane_blk(HIST), lane_blk(dec_seq),
                  pl.BlockSpec((CONV_WIDTH, LANES), lambda c: (0, c)),
                  pl.BlockSpec((1, LANES), lambda c: (0, c))],
        out_specs=lane_blk(dec_seq),
        out_shape=jax.ShapeDtypeStruct((dec_seq, n_seq, D_MODEL), F32),
        compiler_params=_params("arbitrary"),
        name="conv_sample",
    )(hist_t, glu_t, cw, cb)


def _post_sample_body(t_per_step, n_seq, y_ref, sz_ref, x_ref, lng_ref, lnb_ref, wout_ref,
                      kvg_ref, wkv_ref, wkvt_ref, bg_ref, bwz_ref, bwq_ref,
                      x1_ref, q_ref, sz2_ref, k_ref, v_ref, kt_ref, vt_ref):
    x1, sz2, xkv, xb = _post_conv(y_ref[...], sz_ref[...], x_ref[...], lng_ref[...], lnb_ref[...], wout_ref,
                                  kvg_ref[...], bg_ref[...], bwz_ref)
    x1_ref[...] = x1
    sz2_ref[...] = sz2.astype(BF16)
    q_ref[...] = _dot(xb, bwq_ref[...])
    kv = _dot(xkv, wkv_ref[...])
    k_ref[...] = kv[:, :D_MODEL]
    v_ref[...] = kv[:, D_MODEL:]
    kvt = _dot_nt(wkvt_ref[...], xkv)
    for i in range(t_per_step):
        kt_ref[i] = kvt[:D_MODEL, i * n_seq:(i + 1) * n_seq]
        vt_ref[i] = kvt[D_MODEL:, i * n_seq:(i + 1) * n_seq]


def _post_sample(y, sz, x, ln_g, ln_b, w_out_bf, kv_g, w_kv_bf, w_kvt_bf, b_g, bw_z_bf, bw_q_bf,
                 n_seq, rows_per_step):
    n_rows = y.shape[0]
    dec_seq = n_rows // n_seq
    t_per_step = rows_per_step // n_seq
    assert n_seq % LANES == 0 and rows_per_step % n_seq == 0
    row_spec = pl.BlockSpec((rows_per_step, D_MODEL), lambda i: (i, 0))
    t_spec = pl.BlockSpec((t_per_step, D_MODEL, n_seq), lambda i: (i, 0, 0))
    vec = _const_spec((1, D_MODEL))
    square = _const_spec((D_MODEL, D_MODEL))
    f32_rows = jax.ShapeDtypeStruct((n_rows, D_MODEL), F32)
    f32_t = jax.ShapeDtypeStruct((dec_seq, D_MODEL, n_seq), F32)
    return pl.pallas_call(
        functools.partial(_post_sample_body, t_per_step, n_seq),
        grid=(n_rows // rows_per_step,),
        in_specs=[row_spec, row_spec, row_spec, vec, vec, square, vec,
                  _const_spec((D_MODEL, 2 * D_MODEL)), _const_spec((2 * D_MODEL, D_MODEL)),
                  vec, square, square],
        out_specs=[row_spec] * 5 + [t_spec] * 2,
        out_shape=[f32_rows, f32_rows, jax.ShapeDtypeStruct((n_rows, D_MODEL), BF16), f32_rows, f32_rows]
                  + [f32_t] * 2,
        compiler_params=_params("arbitrary"),
        name="post_sample",
    )(y, sz, x, ln_g, ln_b, w_out_bf, kv_g, w_kv_bf, w_kvt_bf, b_g, bw_z_bf, bw_q_bf)


_HEADS_PER_STEP = 8
_KEY_ROWS = 128
_PV_ROWS = HEAD_DIM + BF16_ROWS


def _softmax_probs(t, shift, m_ref, h, first):
    m_cur = jnp.max(t, axis=0, keepdims=True) - shift
    if first:
        m_new, alpha = m_cur, None
    else:
        m_prev = m_ref[h, 0:1, :]
        m_new = jnp.maximum(m_prev, m_cur)
        alpha = jnp.exp2(m_prev - m_new)
    m_ref[h] = jnp.broadcast_to(m_new, (SUBLANES, t.shape[1]))
    return jnp.exp2(t - (m_new + shift)).astype(BF16), alpha


def _attend_prompt_body(n_blocks, slopes_ref, qt_ref, km_ref, k_ref, vt_ref, o_ref,
                        qaug_ref, sd_ref, m_ref, acc_ref):
    g = pl.program_id(1)
    i = pl.program_id(2)
    heads = range(_HEADS_PER_STEP)
    key = lax.broadcasted_iota(jnp.int32, (MOBA_BLOCK, MOBA_BLOCK), 0)
    qry = lax.broadcasted_iota(jnp.int32, (MOBA_BLOCK, MOBA_BLOCK), 1)
    dist = (qry - key).astype(F32)
    blk = lax.broadcasted_iota(jnp.int32, (n_blocks, MOBA_BLOCK), 0)
    feat_head = _idiv(lax.broadcasted_iota(jnp.int32, (LANES, MOBA_BLOCK), 0), HEAD_DIM)
    lane_head = _idiv(lax.broadcasted_iota(jnp.int32, (n_blocks, LANES), 1), HEAD_DIM)
    own0 = pl.multiple_of(i * MOBA_BLOCK, MOBA_BLOCK)
    bias_pad = jnp.zeros((LANES - n_blocks, MOBA_BLOCK), F32)
    slope2 = [slopes_ref[g * _HEADS_PER_STEP + h] * LOG2E for h in heads]

    def pair_rows(h):
        return slice((h // 2) * LANES, (h // 2 + 1) * LANES)

    def v_aug(h, off, width=MOBA_BLOCK):
        return jnp.concatenate([vt_ref[0, h * HEAD_DIM:(h + 1) * HEAD_DIM, pl.ds(off, width)],
                                jnp.ones((BF16_ROWS, width), BF16)], axis=0)

    pair_scores = []
    for pair in range(_HEADS_PER_STEP // 2):
        km_pair = km_ref[0, :, pair_rows(2 * pair)]
        km_heads = jnp.concatenate([jnp.where(lane_head == e, km_pair, 0.0) for e in range(2)], axis=0)
        pair_scores.append(_dot(km_heads, qt_ref[0, pair_rows(2 * pair), :], precision=lax.Precision.HIGHEST))
    q_own = []
    for h in heads:
        scores_h = pair_scores[h // 2][(h % 2) * n_blocks:(h % 2 + 1) * n_blocks]
        chosen = _rank_is_top(scores_h, blk < i, MOBA_TOPK)
        bias = jnp.where(chosen, 0.0, MASK_VALUE)
        q_h = jnp.where(feat_head == h % 2, qt_ref[0, pair_rows(h), :] * (HEAD_DIM ** -0.5 * LOG2E), 0.0).astype(BF16)
        qaug_ref[h] = jnp.concatenate([q_h, jnp.concatenate([bias, bias_pad], axis=0).astype(BF16)], axis=0)
        q_own.append(q_h)

    @pl.when(i == 0)
    def _():
        for h in heads:
            sd_ref[h] = slope2[h] * dist

    s_own = [_dot(k_ref[0, pl.ds(own0, MOBA_BLOCK), pair_rows(h)], q_own[h]) for h in heads]
    p_own = [_softmax_probs(jnp.where(qry >= key, s_own[h] - sd_ref[h], MASK_VALUE), 0.0, m_ref, h, first=True)[0]
             for h in heads]
    for h in heads:
        acc_ref[h] = _dot(v_aug(h, own0), p_own[h])

    lane = lax.broadcasted_iota(jnp.int32, (_KEY_ROWS, LANES), 1)

    def past_block(j, carry):
        off = pl.multiple_of(j * MOBA_BLOCK, MOBA_BLOCK)
        marker = jnp.where(lane == j, 1.0, 0.0).astype(BF16)
        gap = ((i - j) * MOBA_BLOCK).astype(F32)
        parts = [(h, c) for h in heads for c in range(MOBA_BLOCK // _KEY_ROWS)]
        s = [_dot(jnp.concatenate([k_ref[0, pl.ds(off + c * _KEY_ROWS, _KEY_ROWS), pair_rows(h)], marker], axis=1),
                  qaug_ref[h]) for h, c in parts]
        probs = [_softmax_probs(s[n] - sd_ref[h, c * _KEY_ROWS:(c + 1) * _KEY_ROWS, :], slope2[h] * gap, m_ref, h,
                                first=False) for n, (h, c) in enumerate(parts)]
        for n, (h, c) in enumerate(parts):
            p, alpha = probs[n]
            acc_ref[h] = alpha * acc_ref[h] + _dot(v_aug(h, off + c * _KEY_ROWS, _KEY_ROWS), p)
        return carry

    lax.fori_loop(0, i, past_block, 0)

    for pair in range(_HEADS_PER_STEP // 2):
        outs = []
        for e in range(2):
            acc = acc_ref[2 * pair + e]
            outs.append(acc[:HEAD_DIM] / acc[HEAD_DIM:HEAD_DIM + 1])
        o_ref[0, :, pair * LANES:(pair + 1) * LANES] = jnp.concatenate(outs, axis=0).T.astype(BF16)


def _attend_prompt(qt, kmean, k_bf, vt_bf, slopes):
    n_seq, _, seq = qt.shape
    n_blocks = seq // MOBA_BLOCK
    width = _HEADS_PER_STEP * HEAD_DIM
    assert n_blocks <= LANES and n_blocks % SUBLANES == 0 and 2 * HEAD_DIM == LANES
    return pl.pallas_call(
        functools.partial(_attend_prompt_body, n_blocks),
        grid=(n_seq, D_MODEL // width, n_blocks),
        in_specs=[pl.BlockSpec(memory_space=pltpu.SMEM),
                  pl.BlockSpec((1, width, MOBA_BLOCK), lambda b, g, i: (b, g, i)),
                  pl.BlockSpec((1, n_blocks, width), lambda b, g, i: (b, 0, g)),
                  pl.BlockSpec((1, seq, width), lambda b, g, i: (b, 0, g)),
                  pl.BlockSpec((1, width, seq), lambda b, g, i: (b, g, 0))],
        out_specs=pl.BlockSpec((1, MOBA_BLOCK, width), lambda b, g, i: (b, i, g)),
        out_shape=jax.ShapeDtypeStruct((n_seq, seq, D_MODEL), BF16),
        scratch_shapes=[pltpu.VMEM((_HEADS_PER_STEP, 2 * LANES, MOBA_BLOCK), BF16),
                        pltpu.VMEM((_HEADS_PER_STEP, MOBA_BLOCK, MOBA_BLOCK), F32),
                        pltpu.VMEM((_HEADS_PER_STEP, SUBLANES, MOBA_BLOCK), F32),
                        pltpu.VMEM((_HEADS_PER_STEP, _PV_ROWS, MOBA_BLOCK), F32)],
        compiler_params=_params("arbitrary", "arbitrary", "arbitrary"),
        name="moba_attend_prompt",
    )(slopes, qt, kmean, k_bf, vt_bf)


_BLOCKS_PER_STEP = 8
_PAGES_PER_BLOCK = MOBA_BLOCK // PAGE_SIZE


def _attend_sample_body(dec_seq, n_past, past_len, pt_ref, q_ref, slope_ref, *refs):
    n_pg = _BLOCKS_PER_STEP * _PAGES_PER_BLOCK
    k_pages, v_pages = refs[:n_pg], refs[n_pg:2 * n_pg]
    kn_ref, vn_ref, o_ref, qdt_ref, qdb_ref, sc_ref, m_ref, l_ref, acc_ref = refs[2 * n_pg:]
    jj = pl.program_id(1)
    rows = N_HEADS * dec_seq
    lane = lax.broadcasted_iota(jnp.int32, (rows, LANES), 1)
    slope = slope_ref[...]

    @pl.when(jj == 0)
    def _():
        q = q_ref[0]
        q_rows = jnp.concatenate([q] * N_HEADS, axis=0)
        r_head = _idiv(lax.broadcasted_iota(jnp.int32, (rows, D_MODEL), 0), dec_seq)
        l_head = _idiv(lax.broadcasted_iota(jnp.int32, (rows, D_MODEL), 1), HEAD_DIM)
        q_diag = jnp.where(r_head == l_head, q_rows, 0.0)
        qdt_ref[...] = q_diag.T
        qdb_ref[...] = (q_diag * (HEAD_DIM ** -0.5)).astype(BF16)
        m_ref[...] = jnp.zeros((rows, LANES), F32)
        l_ref[...] = jnp.zeros((rows, LANES), F32)

    q_off = _imod(lax.broadcasted_iota(jnp.int32, (rows, MOBA_BLOCK), 0), dec_seq)
    k_off = lax.broadcasted_iota(jnp.int32, (rows, MOBA_BLOCK), 1)
    blocks = range(_BLOCKS_PER_STEP)
    js = [jj * _BLOCKS_PER_STEP + b for b in blocks]

    def block_of(page_refs, b):
        return jnp.concatenate([r[0] for r in page_refs[b * _PAGES_PER_BLOCK:(b + 1) * _PAGES_PER_BLOCK]], axis=1)

    s = []
    for b in blocks:
        kt_blk = block_of(k_pages, b)
        s.append(_dot(qdb_ref[...], kt_blk.astype(BF16)))
        k_mean = jnp.sum(kt_blk, axis=1, keepdims=True) * (1.0 / MOBA_BLOCK)
        sc_ref[pl.ds(js[b], 1), :] = jnp.sum(qdt_ref[...] * k_mean, axis=0, keepdims=True)
    p = []
    m_all, l_all = m_ref[...], l_ref[...]
    for b in blocks:
        t = s[b] - slope * (q_off - k_off + (past_len - js[b] * MOBA_BLOCK)).astype(F32)
        m_j = jnp.max(t, axis=1, keepdims=True)
        e = jnp.exp(t - m_j)
        m_all = jnp.where(lane == js[b], m_j, m_all)
        l_all = jnp.where(lane == js[b], jnp.sum(e, axis=1, keepdims=True), l_all)
        p.append(e.astype(BF16))
    m_ref[...] = m_all
    l_ref[...] = l_all
    for b in blocks:
        acc_ref[js[b]] = _dot_nt(p[b], block_of(v_pages, b).astype(BF16))

    @pl.when(jj == n_past // _BLOCKS_PER_STEP - 1)
    def _():
        pad = jnp.zeros((LANES - dec_seq, D_MODEL), F32)
        k_new = jnp.concatenate([kn_ref[0], pad], axis=0).astype(BF16)
        v_new = jnp.concatenate([vn_ref[0], pad], axis=0).astype(BF16)
        s_own = _dot_nt(qdb_ref[...], k_new)
        d_own = _imod(lax.broadcasted_iota(jnp.int32, (rows, LANES), 0), dec_seq) - lane
        t_own = jnp.where(d_own >= 0, s_own - slope * d_own.astype(F32), MASK_VALUE)
        m_own = jnp.max(t_own, axis=1, keepdims=True)
        p_own = jnp.exp(t_own - m_own)
        l_own = jnp.sum(p_own, axis=1, keepdims=True)
        acc_own = _dot(p_own.astype(BF16), v_new)

        always = lax.broadcasted_iota(jnp.int32, (n_past, rows), 0) >= 0
        chosen_t = jnp.where(_rank_is_top(sc_ref[...], always, min(MOBA_TOPK, n_past)), 1.0, 0.0)
        chosen = jnp.concatenate([chosen_t, jnp.zeros((LANES - n_past, rows), F32)], axis=0).T > 0.5

        m_all = m_ref[...]
        m_top = jnp.maximum(jnp.max(jnp.where(chosen, m_all, MASK_VALUE), axis=1, keepdims=True), m_own)
        w_all = jnp.where(chosen, jnp.exp(m_all - m_top), 0.0)
        w_own = jnp.exp(m_own - m_top)
        denom = jnp.sum(w_all * l_ref[...], axis=1, keepdims=True) + w_own * l_own
        num = w_own * acc_own
        for b in range(n_past):
            num = num + w_all[:, b:b + 1] * acc_ref[b]
        out = num / denom
        l_head = _idiv(lax.broadcasted_iota(jnp.int32, (dec_seq, D_MODEL), 1), HEAD_DIM)
        res = jnp.zeros((dec_seq, D_MODEL), F32)
        for h in range(N_HEADS):
            res = jnp.where(l_head == h, out[h * dec_seq:(h + 1) * dec_seq, :], res)
        o_ref[0] = res


def _attend_sample(q, k_new, v_new, cache_kt, cache_vt, page_table, slope_rows):
    n_seq, dec_seq, _ = q.shape
    n_pages = page_table.shape[1]
    pages_per_step = _BLOCKS_PER_STEP * _PAGES_PER_BLOCK
    assert n_pages % pages_per_step == 0 and dec_seq % SUBLANES == 0
    n_past = n_pages // _PAGES_PER_BLOCK
    past_len = n_pages * PAGE_SIZE
    rows = N_HEADS * dec_seq
    assert n_past % SUBLANES == 0 and n_past <= LANES and dec_seq <= LANES and rows % LANES == 0
    new_spec = pl.BlockSpec((1, dec_seq, D_MODEL), lambda b, jj, pt: (b, 0, 0))

    def page_spec(which):
        return pl.BlockSpec((1, D_MODEL, PAGE_SIZE), lambda b, jj, pt: (pt[b, pages_per_step * jj + which], 0, 0))

    page_specs = [page_spec(w) for w in range(pages_per_step)]
    stats = pltpu.VMEM((rows, LANES), F32)
    grid_spec = pltpu.PrefetchScalarGridSpec(
        num_scalar_prefetch=1,
        grid=(n_seq, n_pages // pages_per_step),
        in_specs=[new_spec, pl.BlockSpec((rows, 1), lambda b, jj, pt: (0, 0))] + page_specs + page_specs
                 + [new_spec, new_spec],
        out_specs=new_spec,
        scratch_shapes=[pltpu.VMEM((D_MODEL, rows), F32), pltpu.VMEM((rows, D_MODEL), BF16),
                        pltpu.VMEM((n_past, rows), F32), stats, stats,
                        pltpu.VMEM((n_past, rows, D_MODEL), F32)],
    )
    return pl.pallas_call(
        functools.partial(_attend_sample_body, dec_seq, n_past, past_len),
        grid_spec=grid_spec,
        out_shape=jax.ShapeDtypeStruct((n_seq, dec_seq, D_MODEL), F32),
        compiler_params=_params("arbitrary", "arbitrary"),
        name="moba_attend_sample",
    )(page_table, q, slope_rows, *([cache_kt] * pages_per_step), *([cache_vt] * pages_per_step), k_new, v_new)


def _out_body(o_ref, sz2_ref, x1_ref, w_ref, g_ref, y_ref):
    x2 = x1_ref[...] + _dot(o_ref[...] * sz2_ref[...], w_ref[...])
    y_ref[...] = x2 * lax.rsqrt(jnp.mean(x2 * x2, axis=-1, keepdims=True) + NORM_EPS) * g_ref[...]


def _out(o, sz2, x1, w_out_bf, final_g, rows_per_step):
    n_rows = o.shape[0]
    row_spec = pl.BlockSpec((rows_per_step, D_MODEL), lambda i: (i, 0))
    return pl.pallas_call(
        _out_body,
        grid=(n_rows // rows_per_step,),
        in_specs=[row_spec, row_spec, row_spec, _const_spec((D_MODEL, D_MODEL)), _const_spec((1, D_MODEL))],
        out_specs=row_spec,
        out_shape=jax.ShapeDtypeStruct((n_rows, D_MODEL), F32),
        compiler_params=_params("arbitrary"),
        name="out",
    )(o, sz2, x1, w_out_bf, final_g)


_ROWS_PER_STEP = 256
_OUT_ROWS_PER_STEP = 512


def kernel(x_prompt, x_sample, state_conv, cache_k, cache_v, page_table, a_norm_g, a_w_in, a_conv_w, a_conv_b,
           a_ln_g, a_ln_b, a_w_out, kv_norm_g, w_kv, b_norm_g, b_w_in, b_w_out, final_norm_g):
    n_seq, seq, _ = x_prompt.shape
    n_dec, dec_seq, _ = x_sample.shape
    assert a_w_in.shape[0] == 1 and b_w_in.shape[0] == 1, "one conv layer and one attention layer"
    assert seq % MOBA_BLOCK == 0 and seq >= HIST

    vec = lambda a: a.reshape(1, D_MODEL)
    a_g, cb, ln_g, ln_b = vec(a_norm_g[0]), vec(a_conv_b[0]), vec(a_ln_g[0]), vec(a_ln_b[0])
    kv_g, b_g, fin_g = vec(kv_norm_g), vec(b_norm_g[0]), vec(final_norm_g)
    cw = a_conv_w[0]
    w_in_bf, w_out_bf = a_w_in[0].astype(BF16), a_w_out[0].astype(BF16)
    w_kv_bf, w_kvt_bf = w_kv.astype(BF16), w_kv.T.astype(BF16)
    bw_q_bf, bw_z_bf = b_w_in[0, :, :D_MODEL].astype(BF16), b_w_in[0, :, D_MODEL:].astype(BF16)
    bw_out_bf = b_w_out[0].astype(BF16)
    slopes = jnp.exp2(-8.0 * jnp.arange(1, N_HEADS + 1, dtype=F32) / N_HEADS)

    xp = x_prompt.reshape(n_seq * seq, D_MODEL)
    glu_p, sz_p = _inproj(xp, a_g, w_in_bf, _ROWS_PER_STEP)
    glu_p3 = glu_p.reshape(n_seq, seq, D_MODEL)
    x1_p, sz2_p, qt_p, kt_p, vt_p, kb_p, vtb_p, kmt_p = _mid_prompt(
        glu_p3, sz_p.reshape(n_seq, seq, D_MODEL), x_prompt,
        cw, cb, ln_g, ln_b, w_out_bf, kv_g, w_kv_bf[:, :D_MODEL], w_kvt_bf, b_g, bw_z_bf, bw_q_bf.T)
    kmean_p = kmt_p.transpose(0, 2, 1)[:, :seq // MOBA_BLOCK]
    o_p = _attend_prompt(qt_p, kmean_p, kb_p, vtb_p, slopes)
    y_p = _out(o_p.reshape(n_seq * seq, D_MODEL), sz2_p.reshape(n_seq * seq, D_MODEL),
               x1_p.reshape(n_seq * seq, D_MODEL), bw_out_bf, fin_g, _OUT_ROWS_PER_STEP)

    n_rows = n_dec * dec_seq
    xs = x_sample.transpose(1, 0, 2).reshape(n_rows, D_MODEL)
    glu_s, sz_s = _inproj(xs, a_g, w_in_bf, _ROWS_PER_STEP)
    glu_s3 = glu_s.reshape(dec_seq, n_dec, D_MODEL)
    hist_t = state_conv[0].transpose(1, 0, 2)
    y_s = _conv_sample(hist_t, glu_s3, cw, cb)
    x1_s, q_s, sz2_s, k_s, v_s, kt_s, vt_s = _post_sample(
        y_s.reshape(n_rows, D_MODEL), sz_s, xs, ln_g, ln_b, w_out_bf, kv_g, w_kv_bf, w_kvt_bf, b_g,
        bw_z_bf, bw_q_bf, n_dec, _ROWS_PER_STEP)
    by_seq = lambda a: a.reshape(dec_seq, n_dec, D_MODEL).transpose(1, 0, 2)
    slope_rows = jnp.repeat(slopes, dec_seq).reshape(N_HEADS * dec_seq, 1)
    pages = cache_k.shape[0]
    cache_kt = cache_k.transpose(0, 2, 3, 1).reshape(pages, D_MODEL, PAGE_SIZE)
    cache_vt = cache_v.transpose(0, 2, 3, 1).reshape(pages, D_MODEL, PAGE_SIZE)
    o_s = _attend_sample(by_seq(q_s), by_seq(k_s), by_seq(v_s), cache_kt, cache_vt, page_table, slope_rows)
    y_s = _out(o_s.astype(BF16).transpose(1, 0, 2).reshape(n_rows, D_MODEL), sz2_s, x1_s, bw_out_bf, fin_g,
               _OUT_ROWS_PER_STEP)

    hist_new = jnp.concatenate([hist_t[dec_seq:], glu_s3], axis=0)[-HIST:]
    heads_p = lambda a: a.reshape(n_seq, N_HEADS, HEAD_DIM, seq).transpose(0, 3, 1, 2)
    heads_s = lambda a: a.reshape(dec_seq, N_HEADS, HEAD_DIM, n_dec).transpose(3, 0, 1, 2)
    return (y_p.reshape(n_seq, seq, D_MODEL), by_seq(y_s),
            glu_p3[:, seq - HIST:][None], hist_new.transpose(1, 0, 2)[None],
            heads_p(kt_p), heads_p(vt_p), heads_s(kt_s), heads_s(vt_s))
```

```python
import functools
import math

import jax
import jax.numpy as jnp
from jax import lax
from jax.experimental import pallas as pl
from jax.experimental.pallas import tpu as pltpu

D_MODEL = 1024
N_HEADS = 16
HEAD_DIM = 64
CONV_WIDTH = 31
HIST = CONV_WIDTH - 1
MOBA_BLOCK = 256
MOBA_TOPK = 3
PAGE_SIZE = 128
NORM_EPS = 1e-6
MASK_VALUE = -1e30

LANES = 128
SUBLANES = 8
BF16_ROWS = 16
VMEM_LIMIT = 56 * 1024 * 1024
LOG2E = math.log2(math.e)

F32 = jnp.float32
BF16 = jnp.bfloat16


def _idiv(x, n):
    assert n & (n - 1) == 0
    return lax.shift_right_logical(x, n.bit_length() - 1)


def _imod(x, n):
    assert n & (n - 1) == 0
    return lax.bitwise_and(x, n - 1)


def _sigmoid(x):
    return 1.0 / (1.0 + jnp.exp(-x))


def _dot(a, b, precision=None):
    return jnp.dot(a, b, preferred_element_type=F32, precision=precision)


def _dot_nt(a, b):
    return lax.dot_general(a, b, (((1,), (1,)), ((), ())), preferred_element_type=F32)


def _params(*semantics):
    return pltpu.CompilerParams(dimension_semantics=semantics,
                                vmem_limit_bytes=VMEM_LIMIT)


def _const_spec(shape):
    zeros = (0,) * len(shape)
    return pl.BlockSpec(shape, lambda *_: zeros, pipeline_mode=pl.Buffered(1))


def _out_rows(o_bf, sz2_bf, x1, w_ref, final_g):
    x2 = x1 + _dot(o_bf * sz2_bf, w_ref[...])
    return x2 * lax.rsqrt(jnp.mean(x2 * x2, axis=-1, keepdims=True) + NORM_EPS) * final_g


def _rank_is_top(scores, valid, n_top):
    n = scores.shape[0]
    row = lax.broadcasted_iota(jnp.int32, scores.shape, 0)
    s = jnp.where(valid, scores, MASK_VALUE)
    rank = jnp.zeros(scores.shape, jnp.int32)
    for j in range(n):
        sj = s[j:j + 1, :]
        rank = rank + jnp.where(row > j, jnp.where(sj >= s, 1, 0), jnp.where(sj > s, 1, 0))
    return jnp.where(valid, rank, n_top) < n_top


def _inproj_body(x_ref, g_ref, w_ref, glu_ref, sz_ref):
    x = x_ref[...]
    h = x * lax.rsqrt(jnp.mean(x * x, axis=-1, keepdims=True) + NORM_EPS) * g_ref[...]
    hb = h.astype(BF16)
    a = _dot(hb, w_ref[:, 0:D_MODEL])
    b = _dot(hb, w_ref[:, D_MODEL:2 * D_MODEL])
    z = _dot(hb, w_ref[:, 2 * D_MODEL:3 * D_MODEL])
    glu_ref[...] = a * _sigmoid(b)
    sz_ref[...] = z * _sigmoid(z)


def _inproj(x, norm_g, w_in_bf, rows_per_step):
    n_rows = x.shape[0]
    row_spec = pl.BlockSpec((rows_per_step, D_MODEL), lambda i: (i, 0))
    return pl.pallas_call(
        _inproj_body,
        grid=(n_rows // rows_per_step,),
        in_specs=[row_spec, _const_spec((1, D_MODEL)), _const_spec((D_MODEL, 3 * D_MODEL))],
        out_specs=[row_spec, row_spec],
        out_shape=[jax.ShapeDtypeStruct((n_rows, D_MODEL), F32)] * 2,
        compiler_params=_params("arbitrary"),
        name="inproj",
    )(x, norm_g, w_in_bf)


def _post_conv(y, sz, x, ln_g, ln_b, w_out_ref, kv_g, b_g, bw_z_ref):
    mu = jnp.mean(y, axis=-1, keepdims=True)
    yc = y - mu
    var = jnp.mean(yc * yc, axis=-1, keepdims=True)
    ln = yc * lax.rsqrt(var + NORM_EPS) * ln_g + ln_b
    u = ln * _sigmoid(ln) * sz
    x1 = x + _dot(u.astype(BF16), w_out_ref[...])
    xn = x1 * lax.rsqrt(jnp.mean(x1 * x1, axis=-1, keepdims=True) + NORM_EPS)
    xb = (xn * b_g).astype(BF16)
    z2 = _dot(xb, bw_z_ref[...])
    return x1, z2 * _sigmoid(z2), (xn * kv_g).astype(BF16), xb


_MID_ROWS = MOBA_BLOCK
_CARRY = 32
_CONV_ROWS = 128
_TAP0 = _CARRY - HIST
_SHIFT_ROWS = _CARRY + _MID_ROWS - SUBLANES


def _mid_prompt_body(glu_ref, sz_ref, x_ref, cw_ref, cb_ref, lng_ref, lnb_ref, wout_ref,
                     kvg_ref, wk_ref, wkvt_ref, bg_ref, bwz_ref, bwqt_ref,
                     x1_ref, sz2_ref, qt_ref, kt_ref, vt_ref, kb_ref, vtb_ref, kmt_ref,
                     full_ref, shift_ref, y_ref):
    t = pl.program_id(1)

    @pl.when(t == 0)
    def _():
        full_ref[0:_CARRY, :] = jnp.zeros((_CARRY, D_MODEL), F32)

    @pl.when(t > 0)
    def _():
        full_ref[0:_CARRY, :] = full_ref[_MID_ROWS:_MID_ROWS + _CARRY, :]

    full_ref[_CARRY:_CARRY + _MID_ROWS, :] = glu_ref[0]

    def lane_group(c, carry):
        off = pl.multiple_of(c * LANES, LANES)
        for s in range(1, SUBLANES):
            shift_ref[s - 1] = full_ref[pl.ds(s, _SHIFT_ROWS), pl.ds(off, LANES)]
        for rc in range(_MID_ROWS // _CONV_ROWS):
            acc = jnp.zeros((_CONV_ROWS, LANES), F32)
            for w in range(CONV_WIDTH):
                r0 = _TAP0 + w + rc * _CONV_ROWS
                s = r0 % SUBLANES
                if s == 0:
                    rows = full_ref[pl.ds(r0, _CONV_ROWS), pl.ds(off, LANES)]
                else:
                    rows = shift_ref[s - 1, pl.ds(r0 - s, _CONV_ROWS), :]
                acc = acc + rows * cw_ref[w:w + 1, pl.ds(off, LANES)]
            y_ref[pl.ds(rc * _CONV_ROWS, _CONV_ROWS), pl.ds(off, LANES)] = acc + cb_ref[:, pl.ds(off, LANES)]
        return carry

    lax.fori_loop(0, D_MODEL // LANES, lane_group, 0)

    x1, sz2, xkv, xb = _post_conv(y_ref[...], sz_ref[0], x_ref[0], lng_ref[...], lnb_ref[...], wout_ref,
                                  kvg_ref[...], bg_ref[...], bwz_ref)
    x1_ref[0] = x1
    sz2_ref[0] = sz2.astype(BF16)
    qt_ref[0] = _dot_nt(bwqt_ref[...], xb)
    kvt = _dot_nt(wkvt_ref[...], xkv)
    kt = kvt[:D_MODEL]
    vt = kvt[D_MODEL:]
    kt_ref[0] = kt
    vt_ref[0] = vt
    vtb_ref[0] = vt.astype(BF16)
    kb_ref[0] = _dot(xkv, wk_ref[...]).astype(BF16)
    mean_col = jnp.sum(kt, axis=1, keepdims=True) * (1.0 / MOBA_BLOCK)
    lane = lax.broadcasted_iota(jnp.int32, (D_MODEL, LANES), 1)

    @pl.when(t == 0)
    def _():
        kmt_ref[0] = jnp.where(lane == 0, mean_col, 0.0)

    @pl.when(t > 0)
    def _():
        kmt_ref[0] = jnp.where(lane == t, mean_col, kmt_ref[0])


def _mid_prompt(glu, sz, x, cw, cb, ln_g, ln_b, w_out_bf, kv_g, w_k_bf, w_kvt_bf, b_g, bw_z_bf, bw_qt_bf):
    n_seq, seq, _ = x.shape
    n_tiles = seq // _MID_ROWS
    assert n_tiles <= LANES
    tile = pl.BlockSpec((1, _MID_ROWS, D_MODEL), lambda b, t: (b, t, 0))
    tile_t = pl.BlockSpec((1, D_MODEL, _MID_ROWS), lambda b, t: (b, 0, t))
    vec = _const_spec((1, D_MODEL))
    square = _const_spec((D_MODEL, D_MODEL))
    f32_rows = jax.ShapeDtypeStruct((n_seq, seq, D_MODEL), F32)
    f32_t = jax.ShapeDtypeStruct((n_seq, D_MODEL, seq), F32)
    return pl.pallas_call(
        _mid_prompt_body,
        grid=(n_seq, n_tiles),
        in_specs=[tile, tile, tile, _const_spec((CONV_WIDTH, D_MODEL)), vec, vec, vec,
                  square, vec, square, _const_spec((2 * D_MODEL, D_MODEL)), vec, square, square],
        out_specs=[tile, tile, tile_t, tile_t, tile_t, tile, tile_t,
                   pl.BlockSpec((1, D_MODEL, LANES), lambda b, t: (b, 0, 0))],
        out_shape=[f32_rows, jax.ShapeDtypeStruct((n_seq, seq, D_MODEL), BF16), f32_t, f32_t, f32_t,
                   jax.ShapeDtypeStruct((n_seq, seq, D_MODEL), BF16),
                   jax.ShapeDtypeStruct((n_seq, D_MODEL, seq), BF16),
                   jax.ShapeDtypeStruct((n_seq, D_MODEL, LANES), F32)],
        scratch_shapes=[pltpu.VMEM((_CARRY + _MID_ROWS, D_MODEL), F32),
                        pltpu.VMEM((SUBLANES - 1, _SHIFT_ROWS, LANES), F32),
                        pltpu.VMEM((_MID_ROWS, D_MODEL), F32)],
        compiler_params=_params("arbitrary", "arbitrary"),
        name="mid_prompt",
    )(glu, sz, x, cw, cb, ln_g, ln_b, w_out_bf, kv_g, w_k_bf, w_kvt_bf, b_g, bw_z_bf, bw_qt_bf)


def _conv_sample_body(dec_seq, hist_ref, glu_ref, cw_ref, cb_ref, y_ref):
    def tap(r):
        return hist_ref[r] if r < HIST else glu_ref[r - HIST]

    for t in range(dec_seq):
        acc = tap(t) * cw_ref[0:1, :]
        for w in range(1, CONV_WIDTH):
            acc = acc + tap(t + w) * cw_ref[w:w + 1, :]
        y_ref[t] = acc + cb_ref[...]


def _conv_sample(hist_t, glu_t, cw, cb):
    dec_seq, n_seq, _ = glu_t.shape
    lane_blk = lambda rows: pl.BlockSpec((rows, n_seq, LANES), lambda c: (0, 0, c))
    return pl.pallas_call(
        functools.partial(_conv_sample_body, dec_seq),
        grid=(D_MODEL // LANES,),
        in_specs=[lane_blk(HIST), lane_blk(dec_seq),
                  pl.BlockSpec((CONV_WIDTH, LANES), lambda c: (0, c)),
                  pl.BlockSpec((1, LANES), lambda c: (0, c))],
        out_specs=lane_blk(dec_seq),
        out_shape=jax.ShapeDtypeStruct((dec_seq, n_seq, D_MODEL), F32),
        compiler_params=_params("arbitrary"),
        name="conv_sample",
    )(hist_t, glu_t, cw, cb)


def _post_sample_body(t_per_step, n_seq, y_ref, sz_ref, x_ref, lng_ref, lnb_ref, wout_ref,
                      kvg_ref, wkv_ref, wkvt_ref, bg_ref, bwz_ref, bwq_ref,
                      x1_ref, q_ref, sz2_ref, k_ref, v_ref, kt_ref, vt_ref):
    x1, sz2, xkv, xb = _post_conv(y_ref[...], sz_ref[...], x_ref[...], lng_ref[...], lnb_ref[...], wout_ref,
                                  kvg_ref[...], bg_ref[...], bwz_ref)
    x1_ref[...] = x1
    sz2_ref[...] = sz2.astype(BF16)
    q_ref[...] = _dot(xb, bwq_ref[...])
    kv = _dot(xkv, wkv_ref[...])
    k_ref[...] = kv[:, :D_MODEL]
    v_ref[...] = kv[:, D_MODEL:]
    kvt = _dot_nt(wkvt_ref[...], xkv)
    for i in range(t_per_step):
        kt_ref[i] = kvt[:D_MODEL, i * n_seq:(i + 1) * n_seq]
        vt_ref[i] = kvt[D_MODEL:, i * n_seq:(i + 1) * n_seq]


def _post_sample(y, sz, x, ln_g, ln_b, w_out_bf, kv_g, w_kv_bf, w_kvt_bf, b_g, bw_z_bf, bw_q_bf,
                 n_seq, rows_per_step):
    n_rows = y.shape[0]
    dec_seq = n_rows // n_seq
    t_per_step = rows_per_step // n_seq
    assert n_seq % LANES == 0 and rows_per_step % n_seq == 0
    row_spec = pl.BlockSpec((rows_per_step, D_MODEL), lambda i: (i, 0))
    t_spec = pl.BlockSpec((t_per_step, D_MODEL, n_seq), lambda i: (i, 0, 0))
    vec = _const_spec((1, D_MODEL))
    square = _const_spec((D_MODEL, D_MODEL))
    f32_rows = jax.ShapeDtypeStruct((n_rows, D_MODEL), F32)
    f32_t = jax.ShapeDtypeStruct((dec_seq, D_MODEL, n_seq), F32)
    return pl.pallas_call(
        functools.partial(_post_sample_body, t_per_step, n_seq),
        grid=(n_rows // rows_per_step,),
        in_specs=[row_spec, row_spec, row_spec, vec, vec, square, vec,
                  _const_spec((D_MODEL, 2 * D_MODEL)), _const_spec((2 * D_MODEL, D_MODEL)),
                  vec, square, square],
        out_specs=[row_spec] * 5 + [t_spec] * 2,
        out_shape=[f32_rows, f32_rows, jax.ShapeDtypeStruct((n_rows, D_MODEL), BF16), f32_rows, f32_rows]
                  + [f32_t] * 2,
        compiler_params=_params("arbitrary"),
        name="post_sample",
    )(y, sz, x, ln_g, ln_b, w_out_bf, kv_g, w_kv_bf, w_kvt_bf, b_g, bw_z_bf, bw_q_bf)


_HEADS_PER_STEP = 16
_KEY_ROWS = 128
_PV_ROWS = HEAD_DIM + BF16_ROWS


def _softmax_probs(t, shift, m_ref, h, first):
    m_cur = jnp.max(t, axis=0, keepdims=True) - shift
    if first:
        m_new, alpha = m_cur, None
    else:
        m_prev = m_ref[h, 0:1, :]
        m_new = jnp.maximum(m_prev, m_cur)
        alpha = jnp.exp2(m_prev - m_new)
    m_ref[h] = jnp.broadcast_to(m_new, (SUBLANES, t.shape[1]))
    return jnp.exp2(t - (m_new + shift)).astype(BF16), alpha


def _attend_prompt_body(n_blocks, slopes_ref, qt_ref, km_ref, k_ref, vt_ref, sz2_ref, x1_ref, wout_ref, fing_ref,
                        y_ref, qaug_ref, sd_ref, m_ref, acc_ref, o_scr):
    i = pl.program_id(1)
    heads = range(_HEADS_PER_STEP)
    key = lax.broadcasted_iota(jnp.int32, (MOBA_BLOCK, MOBA_BLOCK), 0)
    qry = lax.broadcasted_iota(jnp.int32, (MOBA_BLOCK, MOBA_BLOCK), 1)
    dist = (qry - key).astype(F32)
    blk = lax.broadcasted_iota(jnp.int32, (n_blocks, MOBA_BLOCK), 0)
    feat_head = _idiv(lax.broadcasted_iota(jnp.int32, (LANES, MOBA_BLOCK), 0), HEAD_DIM)
    lane_head = _idiv(lax.broadcasted_iota(jnp.int32, (n_blocks, LANES), 1), HEAD_DIM)
    own0 = pl.multiple_of(i * MOBA_BLOCK, MOBA_BLOCK)
    bias_pad = jnp.zeros((LANES - n_blocks, MOBA_BLOCK), F32)
    slope2 = [slopes_ref[h] * LOG2E for h in heads]

    def pair_rows(h):
        return slice((h // 2) * LANES, (h // 2 + 1) * LANES)

    def v_aug(h, off, width=MOBA_BLOCK):
        return jnp.concatenate([vt_ref[0, h * HEAD_DIM:(h + 1) * HEAD_DIM, pl.ds(off, width)],
                                jnp.ones((BF16_ROWS, width), BF16)], axis=0)

    pair_scores = []
    for pair in range(_HEADS_PER_STEP // 2):
        km_pair = km_ref[0, :, pair_rows(2 * pair)]
        km_heads = jnp.concatenate([jnp.where(lane_head == e, km_pair, 0.0) for e in range(2)], axis=0)
        pair_scores.append(_dot(km_heads, qt_ref[0, pair_rows(2 * pair), :], precision=lax.Precision.HIGHEST))
    q_own = []
    for h in heads:
        scores_h = pair_scores[h // 2][(h % 2) * n_blocks:(h % 2 + 1) * n_blocks]
        chosen = _rank_is_top(scores_h, blk < i, MOBA_TOPK)
        bias = jnp.where(chosen, 0.0, MASK_VALUE)
        q_h = jnp.where(feat_head == h % 2, qt_ref[0, pair_rows(h), :] * (HEAD_DIM ** -0.5 * LOG2E), 0.0).astype(BF16)
        qaug_ref[h] = jnp.concatenate([q_h, jnp.concatenate([bias, bias_pad], axis=0).astype(BF16)], axis=0)
        q_own.append(q_h)

    @pl.when(i == 0)
    def _():
        for h in heads:
            sd_ref[h] = slope2[h] * dist

    s_own = [_dot(k_ref[0, pl.ds(own0, MOBA_BLOCK), pair_rows(h)], q_own[h]) for h in heads]
    p_own = [_softmax_probs(jnp.where(qry >= key, s_own[h] - sd_ref[h], MASK_VALUE), 0.0, m_ref, h, first=True)[0]
             for h in heads]
    for h in heads:
        acc_ref[h] = _dot(v_aug(h, own0), p_own[h])

    lane = lax.broadcasted_iota(jnp.int32, (_KEY_ROWS, LANES), 1)

    def past_block(j, carry):
        off = pl.multiple_of(j * MOBA_BLOCK, MOBA_BLOCK)
        marker = jnp.where(lane == j, 1.0, 0.0).astype(BF16)
        gap = ((i - j) * MOBA_BLOCK).astype(F32)
        parts = [(h, c) for h in heads for c in range(MOBA_BLOCK // _KEY_ROWS)]
        s = [_dot(jnp.concatenate([k_ref[0, pl.ds(off + c * _KEY_ROWS, _KEY_ROWS), pair_rows(h)], marker], axis=1),
                  qaug_ref[h]) for h, c in parts]
        probs = [_softmax_probs(s[n] - sd_ref[h, c * _KEY_ROWS:(c + 1) * _KEY_ROWS, :], slope2[h] * gap, m_ref, h,
                                first=False) for n, (h, c) in enumerate(parts)]
        for n, (h, c) in enumerate(parts):
            p, alpha = probs[n]
            acc_ref[h] = alpha * acc_ref[h] + _dot(v_aug(h, off + c * _KEY_ROWS, _KEY_ROWS), p)
        return carry

    lax.fori_loop(0, i, past_block, 0)

    for pair in range(_HEADS_PER_STEP // 2):
        outs = []
        for e in range(2):
            acc = acc_ref[2 * pair + e]
            outs.append(acc[:HEAD_DIM] / acc[HEAD_DIM:HEAD_DIM + 1])
        o_scr[:, pair * LANES:(pair + 1) * LANES] = jnp.concatenate(outs, axis=0).T.astype(BF16)

    y_ref[0] = _out_rows(o_scr[...], sz2_ref[0], x1_ref[0], wout_ref, fing_ref[...])


def _attend_prompt(qt, kmean, k_bf, vt_bf, slopes, sz2, x1, w_out_bf, final_g):
    n_seq, _, seq = qt.shape
    n_blocks = seq // MOBA_BLOCK
    assert n_blocks <= LANES and n_blocks % SUBLANES == 0 and 2 * HEAD_DIM == LANES
    assert _HEADS_PER_STEP == N_HEADS, "the fused out stage needs every head of a query block in one step"
    rows = pl.BlockSpec((1, MOBA_BLOCK, D_MODEL), lambda b, i: (b, i, 0))
    whole = lambda shape: pl.BlockSpec(shape, lambda b, i: (b, 0, 0), pipeline_mode=pl.Buffered(1))
    return pl.pallas_call(
        functools.partial(_attend_prompt_body, n_blocks),
        grid=(n_seq, n_blocks),
        in_specs=[pl.BlockSpec(memory_space=pltpu.SMEM),
                  pl.BlockSpec((1, D_MODEL, MOBA_BLOCK), lambda b, i: (b, 0, i)),
                  whole((1, n_blocks, D_MODEL)), whole((1, seq, D_MODEL)), whole((1, D_MODEL, seq)),
                  rows, rows, _const_spec((D_MODEL, D_MODEL)), _const_spec((1, D_MODEL))],
        out_specs=rows,
        out_shape=jax.ShapeDtypeStruct((n_seq, seq, D_MODEL), F32),
        scratch_shapes=[pltpu.VMEM((_HEADS_PER_STEP, 2 * LANES, MOBA_BLOCK), BF16),
                        pltpu.VMEM((_HEADS_PER_STEP, MOBA_BLOCK, MOBA_BLOCK), F32),
                        pltpu.VMEM((_HEADS_PER_STEP, SUBLANES, MOBA_BLOCK), F32),
                        pltpu.VMEM((_HEADS_PER_STEP, _PV_ROWS, MOBA_BLOCK), F32),
                        pltpu.VMEM((MOBA_BLOCK, D_MODEL), BF16)],
        compiler_params=_params("arbitrary", "arbitrary"),
        name="moba_attend_prompt",
    )(slopes, qt, kmean, k_bf, vt_bf, sz2, x1, w_out_bf, final_g)


_BLOCKS_PER_STEP = 8
_PAGES_PER_BLOCK = MOBA_BLOCK // PAGE_SIZE


def _attend_sample_body(dec_seq, n_past, past_len, pt_ref, q_ref, slope_ref, *refs):
    n_pg = _BLOCKS_PER_STEP * _PAGES_PER_BLOCK
    k_pages, v_pages = refs[:n_pg], refs[n_pg:2 * n_pg]
    kn_ref, vn_ref, o_ref, qdt_ref, qdb_ref, sc_ref, m_ref, l_ref, acc_ref = refs[2 * n_pg:]
    jj = pl.program_id(1)
    rows = N_HEADS * dec_seq
    lane = lax.broadcasted_iota(jnp.int32, (rows, LANES), 1)
    slope = slope_ref[...]

    @pl.when(jj == 0)
    def _():
        q = q_ref[0]
        q_rows = jnp.concatenate([q] * N_HEADS, axis=0)
        r_head = _idiv(lax.broadcasted_iota(jnp.int32, (rows, D_MODEL), 0), dec_seq)
        l_head = _idiv(lax.broadcasted_iota(jnp.int32, (rows, D_MODEL), 1), HEAD_DIM)
        q_diag = jnp.where(r_head == l_head, q_rows, 0.0)
        qdt_ref[...] = q_diag.T
        qdb_ref[...] = (q_diag * (HEAD_DIM ** -0.5)).astype(BF16)
        m_ref[...] = jnp.zeros((rows, LANES), F32)
        l_ref[...] = jnp.zeros((rows, LANES), F32)

    q_off = _imod(lax.broadcasted_iota(jnp.int32, (rows, MOBA_BLOCK), 0), dec_seq)
    k_off = lax.broadcasted_iota(jnp.int32, (rows, MOBA_BLOCK), 1)
    blocks = range(_BLOCKS_PER_STEP)
    js = [jj * _BLOCKS_PER_STEP + b for b in blocks]

    def block_of(page_refs, b):
        return jnp.concatenate([r[0] for r in page_refs[b * _PAGES_PER_BLOCK:(b + 1) * _PAGES_PER_BLOCK]], axis=1)

    s = []
    for b in blocks:
        kt_blk = block_of(k_pages, b)
        s.append(_dot(qdb_ref[...], kt_blk.astype(BF16)))
        k_mean = jnp.sum(kt_blk, axis=1, keepdims=True) * (1.0 / MOBA_BLOCK)
        sc_ref[pl.ds(js[b], 1), :] = jnp.sum(qdt_ref[...] * k_mean, axis=0, keepdims=True)
    p = []
    m_all, l_all = m_ref[...], l_ref[...]
    for b in blocks:
        t = s[b] - slope * (q_off - k_off + (past_len - js[b] * MOBA_BLOCK)).astype(F32)
        m_j = jnp.max(t, axis=1, keepdims=True)
        e = jnp.exp(t - m_j)
        m_all = jnp.where(lane == js[b], m_j, m_all)
        l_all = jnp.where(lane == js[b], jnp.sum(e, axis=1, keepdims=True), l_all)
        p.append(e.astype(BF16))
    m_ref[...] = m_all
    l_ref[...] = l_all
    for b in blocks:
        acc_ref[js[b]] = _dot_nt(p[b], block_of(v_pages, b).astype(BF16))

    @pl.when(jj == n_past // _BLOCKS_PER_STEP - 1)
    def _():
        pad = jnp.zeros((LANES - dec_seq, D_MODEL), F32)
        k_new = jnp.concatenate([kn_ref[0], pad], axis=0).astype(BF16)
        v_new = jnp.concatenate([vn_ref[0], pad], axis=0).astype(BF16)
        s_own = _dot_nt(qdb_ref[...], k_new)
        d_own = _imod(lax.broadcasted_iota(jnp.int32, (rows, LANES), 0), dec_seq) - lane
        t_own = jnp.where(d_own >= 0, s_own - slope * d_own.astype(F32), MASK_VALUE)
        m_own = jnp.max(t_own, axis=1, keepdims=True)
        p_own = jnp.exp(t_own - m_own)
        l_own = jnp.sum(p_own, axis=1, keepdims=True)
        acc_own = _dot(p_own.astype(BF16), v_new)

        always = lax.broadcasted_iota(jnp.int32, (n_past, rows), 0) >= 0
        chosen_t = jnp.where(_rank_is_top(sc_ref[...], always, min(MOBA_TOPK, n_past)), 1.0, 0.0)
        chosen = jnp.concatenate([chosen_t, jnp.zeros((LANES - n_past, rows), F32)], axis=0).T > 0.5

        m_all = m_ref[...]
        m_top = jnp.maximum(jnp.max(jnp.where(chosen, m_all, MASK_VALUE), axis=1, keepdims=True), m_own)
        w_all = jnp.where(chosen, jnp.exp(m_all - m_top), 0.0)
        w_own = jnp.exp(m_own - m_top)
        denom = jnp.sum(w_all * l_ref[...], axis=1, keepdims=True) + w_own * l_own
        num = w_own * acc_own
        for b in range(n_past):
            num = num + w_all[:, b:b + 1] * acc_ref[b]
        out = num / denom
        l_head = _idiv(lax.broadcasted_iota(jnp.int32, (dec_seq, D_MODEL), 1), HEAD_DIM)
        res = jnp.zeros((dec_seq, D_MODEL), F32)
        for h in range(N_HEADS):
            res = jnp.where(l_head == h, out[h * dec_seq:(h + 1) * dec_seq, :], res)
        o_ref[0] = res


def _attend_sample(q, k_new, v_new, cache_kt, cache_vt, page_table, slope_rows):
    n_seq, dec_seq, _ = q.shape
    n_pages = page_table.shape[1]
    pages_per_step = _BLOCKS_PER_STEP * _PAGES_PER_BLOCK
    assert n_pages % pages_per_step == 0 and dec_seq % SUBLANES == 0
    n_past = n_pages // _PAGES_PER_BLOCK
    past_len = n_pages * PAGE_SIZE
    rows = N_HEADS * dec_seq
    assert n_past % SUBLANES == 0 and n_past <= LANES and dec_seq <= LANES and rows % LANES == 0
    new_spec = pl.BlockSpec((1, dec_seq, D_MODEL), lambda b, jj, pt: (b, 0, 0))

    def page_spec(which):
        return pl.BlockSpec((1, D_MODEL, PAGE_SIZE), lambda b, jj, pt: (pt[b, pages_per_step * jj + which], 0, 0))

    page_specs = [page_spec(w) for w in range(pages_per_step)]
    stats = pltpu.VMEM((rows, LANES), F32)
    grid_spec = pltpu.PrefetchScalarGridSpec(
        num_scalar_prefetch=1,
        grid=(n_seq, n_pages // pages_per_step),
        in_specs=[new_spec, pl.BlockSpec((rows, 1), lambda b, jj, pt: (0, 0))] + page_specs + page_specs
                 + [new_spec, new_spec],
        out_specs=new_spec,
        scratch_shapes=[pltpu.VMEM((D_MODEL, rows), F32), pltpu.VMEM((rows, D_MODEL), BF16),
                        pltpu.VMEM((n_past, rows), F32), stats, stats,
                        pltpu.VMEM((n_past, rows, D_MODEL), F32)],
    )
    return pl.pallas_call(
        functools.partial(_attend_sample_body, dec_seq, n_past, past_len),
        grid_spec=grid_spec,
        out_shape=jax.ShapeDtypeStruct((n_seq, dec_seq, D_MODEL), F32),
        compiler_params=_params("arbitrary", "arbitrary"),
        name="moba_attend_sample",
    )(page_table, q, slope_rows, *([cache_kt] * pages_per_step), *([cache_vt] * pages_per_step), k_new, v_new)


def _out_body(o_ref, sz2_ref, x1_ref, w_ref, g_ref, y_ref):
    y_ref[...] = _out_rows(o_ref[...], sz2_ref[...], x1_ref[...], w_ref, g_ref[...])


def _out(o, sz2, x1, w_out_bf, final_g, rows_per_step):
    n_rows = o.shape[0]
    row_spec = pl.BlockSpec((rows_per_step, D_MODEL), lambda i: (i, 0))
    return pl.pallas_call(
        _out_body,
        grid=(n_rows // rows_per_step,),
        in_specs=[row_spec, row_spec, row_spec, _const_spec((D_MODEL, D_MODEL)), _const_spec((1, D_MODEL))],
        out_specs=row_spec,
        out_shape=jax.ShapeDtypeStruct((n_rows, D_MODEL), F32),
        compiler_params=_params("arbitrary"),
        name="out",
    )(o, sz2, x1, w_out_bf, final_g)


_ROWS_PER_STEP = 256
_OUT_ROWS_PER_STEP = 512


def kernel(x_prompt, x_sample, state_conv, cache_k, cache_v, page_table, a_norm_g, a_w_in, a_conv_w, a_conv_b,
           a_ln_g, a_ln_b, a_w_out, kv_norm_g, w_kv, b_norm_g, b_w_in, b_w_out, final_norm_g):
    n_seq, seq, _ = x_prompt.shape
    n_dec, dec_seq, _ = x_sample.shape
    assert a_w_in.shape[0] == 1 and b_w_in.shape[0] == 1, "one conv layer and one attention layer"
    assert seq % MOBA_BLOCK == 0 and seq >= HIST

    vec = lambda a: a.reshape(1, D_MODEL)
    a_g, cb, ln_g, ln_b = vec(a_norm_g[0]), vec(a_conv_b[0]), vec(a_ln_g[0]), vec(a_ln_b[0])
    kv_g, b_g, fin_g = vec(kv_norm_g), vec(b_norm_g[0]), vec(final_norm_g)
    cw = a_conv_w[0]
    w_in_bf, w_out_bf = a_w_in[0].astype(BF16), a_w_out[0].astype(BF16)
    w_kv_bf, w_kvt_bf = w_kv.astype(BF16), w_kv.T.astype(BF16)
    bw_q_bf, bw_z_bf = b_w_in[0, :, :D_MODEL].astype(BF16), b_w_in[0, :, D_MODEL:].astype(BF16)
    bw_out_bf = b_w_out[0].astype(BF16)
    slopes = jnp.exp2(-8.0 * jnp.arange(1, N_HEADS + 1, dtype=F32) / N_HEADS)

    xp = x_prompt.reshape(n_seq * seq, D_MODEL)
    glu_p, sz_p = _inproj(xp, a_g, w_in_bf, _ROWS_PER_STEP)
    glu_p3 = glu_p.reshape(n_seq, seq, D_MODEL)
    x1_p, sz2_p, qt_p, kt_p, vt_p, kb_p, vtb_p, kmt_p = _mid_prompt(
        glu_p3, sz_p.reshape(n_seq, seq, D_MODEL), x_prompt,
        cw, cb, ln_g, ln_b, w_out_bf, kv_g, w_kv_bf[:, :D_MODEL], w_kvt_bf, b_g, bw_z_bf, bw_q_bf.T)
    kmean_p = kmt_p.transpose(0, 2, 1)[:, :seq // MOBA_BLOCK]
    y_p = _attend_prompt(qt_p, kmean_p, kb_p, vtb_p, slopes, sz2_p, x1_p, bw_out_bf, fin_g)

    n_rows = n_dec * dec_seq
    xs = x_sample.transpose(1, 0, 2).reshape(n_rows, D_MODEL)
    glu_s, sz_s = _inproj(xs, a_g, w_in_bf, _ROWS_PER_STEP)
    glu_s3 = glu_s.reshape(dec_seq, n_dec, D_MODEL)
    hist_t = state_conv[0].transpose(1, 0, 2)
    y_s = _conv_sample(hist_t, glu_s3, cw, cb)
    x1_s, q_s, sz2_s, k_s, v_s, kt_s, vt_s = _post_sample(
        y_s.reshape(n_rows, D_MODEL), sz_s, xs, ln_g, ln_b, w_out_bf, kv_g, w_kv_bf, w_kvt_bf, b_g,
        bw_z_bf, bw_q_bf, n_dec, _ROWS_PER_STEP)
    by_seq = lambda a: a.reshape(dec_seq, n_dec, D_MODEL).transpose(1, 0, 2)
    slope_rows = jnp.repeat(slopes, dec_seq).reshape(N_HEADS * dec_seq, 1)
    pages = cache_k.shape[0]
    cache_kt = cache_k.transpose(0, 2, 3, 1).reshape(pages, D_MODEL, PAGE_SIZE)
    cache_vt = cache_v.transpose(0, 2, 3, 1).reshape(pages, D_MODEL, PAGE_SIZE)
    o_s = _attend_sample(by_seq(q_s), by_seq(k_s), by_seq(v_s), cache_kt, cache_vt, page_table, slope_rows)
    y_s = _out(o_s.astype(BF16).transpose(1, 0, 2).reshape(n_rows, D_MODEL), sz2_s, x1_s, bw_out_bf, fin_g,
               _OUT_ROWS_PER_STEP)

    hist_new = jnp.concatenate([hist_t[dec_seq:], glu_s3], axis=0)[-HIST:]
    heads_p = lambda a: a.reshape(n_seq, N_HEADS, HEAD_DIM, seq).transpose(0, 3, 1, 2)
    heads_s = lambda a: a.reshape(dec_seq, N_HEADS, HEAD_DIM, n_dec).transpose(3, 0, 1, 2)
    return (y_p.reshape(n_seq, seq, D_MODEL), by_seq(y_s),
            glu_p3[:, seq - HIST:][None], hist_new.transpose(1, 0, 2)[None],
            heads_p(kt_p), heads_p(vt_p), heads_s(kt_s), heads_s(vt_s))
```

```python
import functools
import math

import jax
import jax.numpy as jnp
from jax import lax
from jax.experimental import pallas as pl
from jax.experimental.pallas import tpu as pltpu

D_MODEL = 1024
N_HEADS = 16
HEAD_DIM = 64
CONV_WIDTH = 31
HIST = CONV_WIDTH - 1
MOBA_BLOCK = 256
MOBA_TOPK = 3
PAGE_SIZE = 128
NORM_EPS = 1e-6
MASK_VALUE = -1e30

LANES = 128
SUBLANES = 8
BF16_ROWS = 16
VMEM_LIMIT = 56 * 1024 * 1024
LOG2E = math.log2(math.e)

F32 = jnp.float32
BF16 = jnp.bfloat16


def _idiv(x, n):
    assert n & (n - 1) == 0
    return lax.shift_right_logical(x, n.bit_length() - 1)


def _imod(x, n):
    assert n & (n - 1) == 0
    return lax.bitwise_and(x, n - 1)


def _sigmoid(x):
    return 1.0 / (1.0 + jnp.exp(-x))


def _dot(a, b, precision=None):
    return jnp.dot(a, b, preferred_element_type=F32, precision=precision)


def _dot_nt(a, b):
    return lax.dot_general(a, b, (((1,), (1,)), ((), ())), preferred_element_type=F32)


def _params(*semantics):
    return pltpu.CompilerParams(dimension_semantics=semantics,
                                vmem_limit_bytes=VMEM_LIMIT)


def _const_spec(shape):
    zeros = (0,) * len(shape)
    return pl.BlockSpec(shape, lambda *_: zeros, pipeline_mode=pl.Buffered(1))


def _out_rows(o_bf, sz2_bf, x1, w_ref, final_g):
    x2 = x1 + _dot(o_bf * sz2_bf, w_ref[...])
    return x2 * lax.rsqrt(jnp.mean(x2 * x2, axis=-1, keepdims=True) + NORM_EPS) * final_g


def _rank_is_top(scores, valid, n_top):
    n = scores.shape[0]
    row = lax.broadcasted_iota(jnp.int32, scores.shape, 0).astype(F32)
    left = jnp.where(valid, scores, MASK_VALUE)
    picked = jnp.zeros(scores.shape, F32)
    for _ in range(n_top):
        best = jnp.max(left, axis=0, keepdims=True)
        first = jnp.min(jnp.where(left == best, row, float(n)), axis=0, keepdims=True)
        hit = row == first
        picked = jnp.where(hit, 1.0, picked)
        left = jnp.where(hit, -jnp.inf, left)
    return jnp.where(valid, picked, 0.0) > 0.5


def _inproj_body(x_ref, g_ref, w_ref, glu_ref, sz_ref):
    x = x_ref[...]
    h = x * lax.rsqrt(jnp.mean(x * x, axis=-1, keepdims=True) + NORM_EPS) * g_ref[...]
    hb = h.astype(BF16)
    a = _dot(hb, w_ref[:, 0:D_MODEL])
    b = _dot(hb, w_ref[:, D_MODEL:2 * D_MODEL])
    z = _dot(hb, w_ref[:, 2 * D_MODEL:3 * D_MODEL])
    glu_ref[...] = a * _sigmoid(b)
    sz_ref[...] = z * _sigmoid(z)


def _inproj(x, norm_g, w_in_bf, rows_per_step):
    n_rows = x.shape[0]
    row_spec = pl.BlockSpec((rows_per_step, D_MODEL), lambda i: (i, 0))
    return pl.pallas_call(
        _inproj_body,
        grid=(n_rows // rows_per_step,),
        in_specs=[row_spec, _const_spec((1, D_MODEL)), _const_spec((D_MODEL, 3 * D_MODEL))],
        out_specs=[row_spec, row_spec],
        out_shape=[jax.ShapeDtypeStruct((n_rows, D_MODEL), F32)] * 2,
        compiler_params=_params("arbitrary"),
        name="inproj",
    )(x, norm_g, w_in_bf)


def _post_conv(y, sz, x, ln_g, ln_b, w_out_ref, kv_g, b_g, bw_z_ref):
    mu = jnp.mean(y, axis=-1, keepdims=True)
    yc = y - mu
    var = jnp.mean(yc * yc, axis=-1, keepdims=True)
    ln = yc * lax.rsqrt(var + NORM_EPS) * ln_g + ln_b
    u = ln * _sigmoid(ln) * sz
    x1 = x + _dot(u.astype(BF16), w_out_ref[...])
    xn = x1 * lax.rsqrt(jnp.mean(x1 * x1, axis=-1, keepdims=True) + NORM_EPS)
    xb = (xn * b_g).astype(BF16)
    z2 = _dot(xb, bw_z_ref[...])
    return x1, z2 * _sigmoid(z2), (xn * kv_g).astype(BF16), xb


_MID_ROWS = MOBA_BLOCK
_CARRY = 32
_CONV_ROWS = 128
_TAP0 = _CARRY - HIST
_SHIFT_ROWS = _CARRY + _MID_ROWS - SUBLANES


def _mid_prompt_body(glu_ref, sz_ref, x_ref, cw_ref, cb_ref, lng_ref, lnb_ref, wout_ref,
                     kvg_ref, wkvt_ref, bg_ref, bwz_ref, bwqt_ref,
                     x1_ref, sz2_ref, qt_ref, kt_ref, vt_ref, kb_ref, vtb_ref, kmt_ref,
                     full_ref, shift_ref, y_ref):
    t = pl.program_id(1)

    @pl.when(t == 0)
    def _():
        full_ref[0:_CARRY, :] = jnp.zeros((_CARRY, D_MODEL), F32)

    @pl.when(t > 0)
    def _():
        full_ref[0:_CARRY, :] = full_ref[_MID_ROWS:_MID_ROWS + _CARRY, :]

    full_ref[_CARRY:_CARRY + _MID_ROWS, :] = glu_ref[0]

    def lane_group(c, carry):
        off = pl.multiple_of(c * LANES, LANES)
        for s in range(1, SUBLANES):
            shift_ref[s - 1] = full_ref[pl.ds(s, _SHIFT_ROWS), pl.ds(off, LANES)]
        for rc in range(_MID_ROWS // _CONV_ROWS):
            acc = jnp.zeros((_CONV_ROWS, LANES), F32)
            for s in range(SUBLANES):
                taps = [w for w in range(CONV_WIDTH) if (_TAP0 + w) % SUBLANES == s]
                span = (_TAP0 + taps[-1]) // SUBLANES * SUBLANES + _CONV_ROWS
                if s == 0:
                    rows = full_ref[pl.ds(rc * _CONV_ROWS, span), pl.ds(off, LANES)]
                else:
                    rows = shift_ref[s - 1, pl.ds(rc * _CONV_ROWS, span), :]
                for w in taps:
                    a = (_TAP0 + w) // SUBLANES * SUBLANES
                    acc = acc + rows[a:a + _CONV_ROWS] * cw_ref[w:w + 1, pl.ds(off, LANES)]
            y_ref[pl.ds(rc * _CONV_ROWS, _CONV_ROWS), pl.ds(off, LANES)] = acc + cb_ref[:, pl.ds(off, LANES)]
        return carry

    lax.fori_loop(0, D_MODEL // LANES, lane_group, 0)

    x1, sz2, xkv, xb = _post_conv(y_ref[...], sz_ref[0], x_ref[0], lng_ref[...], lnb_ref[...], wout_ref,
                                  kvg_ref[...], bg_ref[...], bwz_ref)
    x1_ref[0] = x1
    sz2_ref[0] = sz2.astype(BF16)
    qt_ref[0] = _dot_nt(bwqt_ref[...], xb)
    kvt = _dot_nt(wkvt_ref[...], xkv)
    kt = kvt[:D_MODEL]
    vt = kvt[D_MODEL:]
    kt_ref[0] = kt
    vt_ref[0] = vt
    vtb_ref[0] = vt.astype(BF16)
    kb_ref[0] = kt.T.astype(BF16)
    mean_col = jnp.sum(kt, axis=1, keepdims=True) * (1.0 / MOBA_BLOCK)
    lane = lax.broadcasted_iota(jnp.int32, (D_MODEL, LANES), 1)

    @pl.when(t == 0)
    def _():
        kmt_ref[0] = jnp.where(lane == 0, mean_col, 0.0)

    @pl.when(t > 0)
    def _():
        kmt_ref[0] = jnp.where(lane == t, mean_col, kmt_ref[0])


def _mid_prompt(glu, sz, x, cw, cb, ln_g, ln_b, w_out_bf, kv_g, w_kvt_bf, b_g, bw_z_bf, bw_qt_bf):
    n_seq, seq, _ = x.shape
    n_tiles = seq // _MID_ROWS
    assert n_tiles <= LANES
    tile = pl.BlockSpec((1, _MID_ROWS, D_MODEL), lambda b, t: (b, t, 0))
    tile_t = pl.BlockSpec((1, D_MODEL, _MID_ROWS), lambda b, t: (b, 0, t))
    vec = _const_spec((1, D_MODEL))
    square = _const_spec((D_MODEL, D_MODEL))
    f32_rows = jax.ShapeDtypeStruct((n_seq, seq, D_MODEL), F32)
    f32_t = jax.ShapeDtypeStruct((n_seq, D_MODEL, seq), F32)
    return pl.pallas_call(
        _mid_prompt_body,
        grid=(n_seq, n_tiles),
        in_specs=[tile, tile, tile, _const_spec((CONV_WIDTH, D_MODEL)), vec, vec, vec,
                  square, vec, _const_spec((2 * D_MODEL, D_MODEL)), vec, square, square],
        out_specs=[tile, tile, tile_t, tile_t, tile_t, tile, tile_t,
                   pl.BlockSpec((1, D_MODEL, LANES), lambda b, t: (b, 0, 0))],
        out_shape=[f32_rows, jax.ShapeDtypeStruct((n_seq, seq, D_MODEL), BF16), f32_t, f32_t, f32_t,
                   jax.ShapeDtypeStruct((n_seq, seq, D_MODEL), BF16),
                   jax.ShapeDtypeStruct((n_seq, D_MODEL, seq), BF16),
                   jax.ShapeDtypeStruct((n_seq, D_MODEL, LANES), F32)],
        scratch_shapes=[pltpu.VMEM((_CARRY + _MID_ROWS, D_MODEL), F32),
                        pltpu.VMEM((SUBLANES - 1, _SHIFT_ROWS, LANES), F32),
                        pltpu.VMEM((_MID_ROWS, D_MODEL), F32)],
        compiler_params=_params("arbitrary", "arbitrary"),
        name="mid_prompt",
    )(glu, sz, x, cw, cb, ln_g, ln_b, w_out_bf, kv_g, w_kvt_bf, b_g, bw_z_bf, bw_qt_bf)


def _conv_sample_body(dec_seq, hist_ref, glu_ref, cw_ref, cb_ref, y_ref, hist_out_ref):
    def tap(r):
        return hist_ref[r] if r < HIST else glu_ref[r - HIST]

    for t in range(dec_seq):
        acc = tap(t) * cw_ref[0:1, :]
        for w in range(1, CONV_WIDTH):
            acc = acc + tap(t + w) * cw_ref[w:w + 1, :]
        y_ref[t] = acc + cb_ref[...]
    for r in range(HIST):
        hist_out_ref[r] = tap(r + dec_seq)


def _conv_sample(hist_t, glu_t, cw, cb):
    dec_seq, n_seq, _ = glu_t.shape
    lane_blk = lambda rows: pl.BlockSpec((rows, n_seq, LANES), lambda c: (0, 0, c))
    return pl.pallas_call(
        functools.partial(_conv_sample_body, dec_seq),
        grid=(D_MODEL // LANES,),
        in_specs=[lane_blk(HIST), lane_blk(dec_seq),
                  pl.BlockSpec((CONV_WIDTH, LANES), lambda c: (0, c)),
                  pl.BlockSpec((1, LANES), lambda c: (0, c))],
        out_specs=[lane_blk(dec_seq), lane_blk(HIST)],
        out_shape=[jax.ShapeDtypeStruct((dec_seq, n_seq, D_MODEL), F32),
                   jax.ShapeDtypeStruct((HIST, n_seq, D_MODEL), F32)],
        compiler_params=_params("arbitrary"),
        name="conv_sample",
    )(hist_t, glu_t, cw, cb)


def _post_sample_body(t_per_step, n_seq, y_ref, sz_ref, x_ref, lng_ref, lnb_ref, wout_ref,
                      kvg_ref, wkv_ref, wkvt_ref, bg_ref, bwz_ref, bwq_ref,
                      x1_ref, q_ref, sz2_ref, k_ref, v_ref, kt_ref, vt_ref):
    x1, sz2, xkv, xb = _post_conv(y_ref[...], sz_ref[...], x_ref[...], lng_ref[...], lnb_ref[...], wout_ref,
                                  kvg_ref[...], bg_ref[...], bwz_ref)
    x1_ref[...] = x1
    sz2_ref[...] = sz2.astype(BF16)
    q_ref[...] = _dot(xb, bwq_ref[...])
    kv = _dot(xkv, wkv_ref[...])
    k_ref[...] = kv[:, :D_MODEL]
    v_ref[...] = kv[:, D_MODEL:]
    kvt = _dot_nt(wkvt_ref[...], xkv)
    for i in range(t_per_step):
        kt_ref[i] = kvt[:D_MODEL, i * n_seq:(i + 1) * n_seq]
        vt_ref[i] = kvt[D_MODEL:, i * n_seq:(i + 1) * n_seq]


def _post_sample(y, sz, x, ln_g, ln_b, w_out_bf, kv_g, w_kv_bf, w_kvt_bf, b_g, bw_z_bf, bw_q_bf,
                 n_seq, rows_per_step):
    n_rows = y.shape[0]
    dec_seq = n_rows // n_seq
    t_per_step = rows_per_step // n_seq
    assert n_seq % LANES == 0 and rows_per_step % n_seq == 0
    row_spec = pl.BlockSpec((rows_per_step, D_MODEL), lambda i: (i, 0))
    t_spec = pl.BlockSpec((t_per_step, D_MODEL, n_seq), lambda i: (i, 0, 0))
    vec = _const_spec((1, D_MODEL))
    square = _const_spec((D_MODEL, D_MODEL))
    f32_rows = jax.ShapeDtypeStruct((n_rows, D_MODEL), F32)
    f32_t = jax.ShapeDtypeStruct((dec_seq, D_MODEL, n_seq), F32)
    return pl.pallas_call(
        functools.partial(_post_sample_body, t_per_step, n_seq),
        grid=(n_rows // rows_per_step,),
        in_specs=[row_spec, row_spec, row_spec, vec, vec, square, vec,
                  _const_spec((D_MODEL, 2 * D_MODEL)), _const_spec((2 * D_MODEL, D_MODEL)),
                  vec, square, square],
        out_specs=[row_spec] * 5 + [t_spec] * 2,
        out_shape=[f32_rows, f32_rows, jax.ShapeDtypeStruct((n_rows, D_MODEL), BF16), f32_rows, f32_rows]
                  + [f32_t] * 2,
        compiler_params=_params("arbitrary"),
        name="post_sample",
    )(y, sz, x, ln_g, ln_b, w_out_bf, kv_g, w_kv_bf, w_kvt_bf, b_g, bw_z_bf, bw_q_bf)


_HEADS_PER_STEP = 16
_KEY_ROWS = 128
_PV_ROWS = HEAD_DIM + BF16_ROWS


def _softmax_probs(t, shift, m_ref, h, first):
    m_cur = jnp.max(t, axis=0, keepdims=True) - shift
    if first:
        m_new, alpha = m_cur, None
    else:
        m_prev = m_ref[h, 0:1, :]
        m_new = jnp.maximum(m_prev, m_cur)
        alpha = jnp.exp2(m_prev - m_new)
    m_ref[h] = jnp.broadcast_to(m_new, (SUBLANES, t.shape[1]))
    return jnp.exp2(t - (m_new + shift)).astype(BF16), alpha


def _attend_prompt_body(n_blocks, slopes_ref, qt_ref, km_ref, k_ref, vt_ref, sz2_ref, x1_ref, wout_ref, fing_ref,
                        y_ref, qaug_ref, sd_ref, m_ref, acc_ref, o_scr):
    i = pl.program_id(1)
    heads = range(_HEADS_PER_STEP)
    key = lax.broadcasted_iota(jnp.int32, (MOBA_BLOCK, MOBA_BLOCK), 0)
    qry = lax.broadcasted_iota(jnp.int32, (MOBA_BLOCK, MOBA_BLOCK), 1)
    dist = (qry - key).astype(F32)
    blk = lax.broadcasted_iota(jnp.int32, (n_blocks, MOBA_BLOCK), 0)
    feat_head = _idiv(lax.broadcasted_iota(jnp.int32, (LANES, MOBA_BLOCK), 0), HEAD_DIM)
    lane_head = _idiv(lax.broadcasted_iota(jnp.int32, (n_blocks, LANES), 1), HEAD_DIM)
    own0 = pl.multiple_of(i * MOBA_BLOCK, MOBA_BLOCK)
    bias_pad = jnp.zeros((LANES - n_blocks, MOBA_BLOCK), F32)
    slope2 = [slopes_ref[h] * LOG2E for h in heads]

    def pair_rows(h):
        return slice((h // 2) * LANES, (h // 2 + 1) * LANES)

    def v_aug(h, off, width=MOBA_BLOCK):
        return jnp.concatenate([vt_ref[0, h * HEAD_DIM:(h + 1) * HEAD_DIM, pl.ds(off, width)],
                                jnp.ones((BF16_ROWS, width), BF16)], axis=0)

    pair_scores = []
    for pair in range(_HEADS_PER_STEP // 2):
        km_pair = km_ref[0, :, pair_rows(2 * pair)]
        km_heads = jnp.concatenate([jnp.where(lane_head == e, km_pair, 0.0) for e in range(2)], axis=0)
        pair_scores.append(_dot(km_heads, qt_ref[0, pair_rows(2 * pair), :], precision=lax.Precision.HIGHEST))
    q_own = []
    for h in heads:
        scores_h = pair_scores[h // 2][(h % 2) * n_blocks:(h % 2 + 1) * n_blocks]
        chosen = _rank_is_top(scores_h, blk < i, MOBA_TOPK)
        bias = jnp.where(chosen, 0.0, MASK_VALUE)
        q_h = jnp.where(feat_head == h % 2, qt_ref[0, pair_rows(h), :] * (HEAD_DIM ** -0.5 * LOG2E), 0.0).astype(BF16)
        qaug_ref[h] = jnp.concatenate([q_h, jnp.concatenate([bias, bias_pad], axis=0).astype(BF16)], axis=0)
        q_own.append(q_h)

    @pl.when(i == 0)
    def _():
        for h in heads:
            sd_ref[h] = slope2[h] * dist

    s_own = [_dot(k_ref[0, pl.ds(own0, MOBA_BLOCK), pair_rows(h)], q_own[h]) for h in heads]
    p_own = [_softmax_probs(jnp.where(qry >= key, s_own[h] - sd_ref[h], MASK_VALUE), 0.0, m_ref, h, first=True)[0]
             for h in heads]
    for h in heads:
        acc_ref[h] = _dot(v_aug(h, own0), p_own[h])

    lane = lax.broadcasted_iota(jnp.int32, (_KEY_ROWS, LANES), 1)

    def past_block(j, carry):
        off = pl.multiple_of(j * MOBA_BLOCK, MOBA_BLOCK)
        marker = jnp.where(lane == j, 1.0, 0.0).astype(BF16)
        gap = ((i - j) * MOBA_BLOCK).astype(F32)
        parts = [(h, c) for h in heads for c in range(MOBA_BLOCK // _KEY_ROWS)]
        s = [_dot(jnp.concatenate([k_ref[0, pl.ds(off + c * _KEY_ROWS, _KEY_ROWS), pair_rows(h)], marker], axis=1),
                  qaug_ref[h]) for h, c in parts]
        probs = [_softmax_probs(s[n] - sd_ref[h, c * _KEY_ROWS:(c + 1) * _KEY_ROWS, :], slope2[h] * gap, m_ref, h,
                                first=False) for n, (h, c) in enumerate(parts)]
        for n, (h, c) in enumerate(parts):
            p, alpha = probs[n]
            acc_ref[h] = alpha * acc_ref[h] + _dot(v_aug(h, off + c * _KEY_ROWS, _KEY_ROWS), p)
        return carry

    lax.fori_loop(0, i, past_block, 0)

    for pair in range(_HEADS_PER_STEP // 2):
        outs = []
        for e in range(2):
            acc = acc_ref[2 * pair + e]
            outs.append(acc[:HEAD_DIM] / acc[HEAD_DIM:HEAD_DIM + 1])
        o_scr[:, pair * LANES:(pair + 1) * LANES] = jnp.concatenate(outs, axis=0).T.astype(BF16)

    y_ref[0] = _out_rows(o_scr[...], sz2_ref[0], x1_ref[0], wout_ref, fing_ref[...])


def _attend_prompt(qt, kmean, k_bf, vt_bf, slopes, sz2, x1, w_out_bf, final_g):
    n_seq, _, seq = qt.shape
    n_blocks = seq // MOBA_BLOCK
    assert n_blocks <= LANES and n_blocks % SUBLANES == 0 and 2 * HEAD_DIM == LANES
    assert _HEADS_PER_STEP == N_HEADS, "the fused out stage needs every head of a query block in one step"
    rows = pl.BlockSpec((1, MOBA_BLOCK, D_MODEL), lambda b, i: (b, i, 0))
    whole = lambda shape: pl.BlockSpec(shape, lambda b, i: (b, 0, 0), pipeline_mode=pl.Buffered(1))
    return pl.pallas_call(
        functools.partial(_attend_prompt_body, n_blocks),
        grid=(n_seq, n_blocks),
        in_specs=[pl.BlockSpec(memory_space=pltpu.SMEM),
                  pl.BlockSpec((1, D_MODEL, MOBA_BLOCK), lambda b, i: (b, 0, i)),
                  whole((1, n_blocks, D_MODEL)), whole((1, seq, D_MODEL)), whole((1, D_MODEL, seq)),
                  rows, rows, _const_spec((D_MODEL, D_MODEL)), _const_spec((1, D_MODEL))],
        out_specs=rows,
        out_shape=jax.ShapeDtypeStruct((n_seq, seq, D_MODEL), F32),
        scratch_shapes=[pltpu.VMEM((_HEADS_PER_STEP, 2 * LANES, MOBA_BLOCK), BF16),
                        pltpu.VMEM((_HEADS_PER_STEP, MOBA_BLOCK, MOBA_BLOCK), F32),
                        pltpu.VMEM((_HEADS_PER_STEP, SUBLANES, MOBA_BLOCK), F32),
                        pltpu.VMEM((_HEADS_PER_STEP, _PV_ROWS, MOBA_BLOCK), F32),
                        pltpu.VMEM((MOBA_BLOCK, D_MODEL), BF16)],
        compiler_params=_params("arbitrary", "arbitrary"),
        name="moba_attend_prompt",
    )(slopes, qt, kmean, k_bf, vt_bf, sz2, x1, w_out_bf, final_g)


_BLOCKS_PER_STEP = 8
_PAGES_PER_BLOCK = MOBA_BLOCK // PAGE_SIZE


def _attend_sample_body(dec_seq, n_past, past_len, pt_ref, q_ref, slope_ref, *refs):
    n_pg = _BLOCKS_PER_STEP * _PAGES_PER_BLOCK
    k_pages, v_pages = refs[:n_pg], refs[n_pg:2 * n_pg]
    kn_ref, vn_ref, o_ref, qdt_ref, qdb_ref, sc_ref, m_ref, l_ref, acc_ref = refs[2 * n_pg:]
    jj = pl.program_id(1)
    rows = N_HEADS * dec_seq
    lane = lax.broadcasted_iota(jnp.int32, (rows, LANES), 1)
    slope = slope_ref[...]

    @pl.when(jj == 0)
    def _():
        q = q_ref[0]
        q_rows = jnp.concatenate([q] * N_HEADS, axis=0)
        r_head = _idiv(lax.broadcasted_iota(jnp.int32, (rows, D_MODEL), 0), dec_seq)
        l_head = _idiv(lax.broadcasted_iota(jnp.int32, (rows, D_MODEL), 1), HEAD_DIM)
        q_diag = jnp.where(r_head == l_head, q_rows, 0.0)
        qdt_ref[...] = q_diag.T
        qdb_ref[...] = (q_diag * (HEAD_DIM ** -0.5)).astype(BF16)
        m_ref[...] = jnp.zeros((rows, LANES), F32)
        l_ref[...] = jnp.zeros((rows, LANES), F32)

    q_off = _imod(lax.broadcasted_iota(jnp.int32, (rows, MOBA_BLOCK), 0), dec_seq)
    k_off = lax.broadcasted_iota(jnp.int32, (rows, MOBA_BLOCK), 1)
    blocks = range(_BLOCKS_PER_STEP)
    js = [jj * _BLOCKS_PER_STEP + b for b in blocks]

    def block_of(page_refs, b):
        return jnp.concatenate([r[0] for r in page_refs[b * _PAGES_PER_BLOCK:(b + 1) * _PAGES_PER_BLOCK]], axis=1)

    s = []
    for b in blocks:
        kt_blk = block_of(k_pages, b)
        s.append(_dot(qdb_ref[...], kt_blk.astype(BF16)))
        k_mean = jnp.sum(kt_blk, axis=1, keepdims=True) * (1.0 / MOBA_BLOCK)
        sc_ref[pl.ds(js[b], 1), :] = jnp.sum(qdt_ref[...] * k_mean, axis=0, keepdims=True)
    p = []
    m_all, l_all = m_ref[...], l_ref[...]
    for b in blocks:
        t = s[b] - slope * (q_off - k_off + (past_len - js[b] * MOBA_BLOCK)).astype(F32)
        m_j = jnp.max(t, axis=1, keepdims=True)
        e = jnp.exp(t - m_j)
        m_all = jnp.where(lane == js[b], m_j, m_all)
        l_all = jnp.where(lane == js[b], jnp.sum(e, axis=1, keepdims=True), l_all)
        p.append(e.astype(BF16))
    m_ref[...] = m_all
    l_ref[...] = l_all
    for b in blocks:
        acc_ref[js[b]] = _dot_nt(p[b], block_of(v_pages, b).astype(BF16))

    @pl.when(jj == n_past // _BLOCKS_PER_STEP - 1)
    def _():
        pad = jnp.zeros((LANES - dec_seq, D_MODEL), F32)
        k_new = jnp.concatenate([kn_ref[0], pad], axis=0).astype(BF16)
        v_new = jnp.concatenate([vn_ref[0], pad], axis=0).astype(BF16)
        s_own = _dot_nt(qdb_ref[...], k_new)
        d_own = _imod(lax.broadcasted_iota(jnp.int32, (rows, LANES), 0), dec_seq) - lane
        t_own = jnp.where(d_own >= 0, s_own - slope * d_own.astype(F32), MASK_VALUE)
        m_own = jnp.max(t_own, axis=1, keepdims=True)
        p_own = jnp.exp(t_own - m_own)
        l_own = jnp.sum(p_own, axis=1, keepdims=True)
        acc_own = _dot(p_own.astype(BF16), v_new)

        always = lax.broadcasted_iota(jnp.int32, (n_past, rows), 0) >= 0
        chosen_t = jnp.where(_rank_is_top(sc_ref[...], always, min(MOBA_TOPK, n_past)), 1.0, 0.0)
        chosen = jnp.concatenate([chosen_t, jnp.zeros((LANES - n_past, rows), F32)], axis=0).T > 0.5

        m_all = m_ref[...]
        m_top = jnp.maximum(jnp.max(jnp.where(chosen, m_all, MASK_VALUE), axis=1, keepdims=True), m_own)
        w_all = jnp.where(chosen, jnp.exp(m_all - m_top), 0.0)
        w_own = jnp.exp(m_own - m_top)
        denom = jnp.sum(w_all * l_ref[...], axis=1, keepdims=True) + w_own * l_own
        num = w_own * acc_own
        for b in range(n_past):
            num = num + w_all[:, b:b + 1] * acc_ref[b]
        out = num / denom
        l_head = _idiv(lax.broadcasted_iota(jnp.int32, (dec_seq, D_MODEL), 1), HEAD_DIM)
        res = jnp.zeros((dec_seq, D_MODEL), F32)
        for h in range(N_HEADS):
            res = jnp.where(l_head == h, out[h * dec_seq:(h + 1) * dec_seq, :], res)
        o_ref[0] = res


def _attend_sample(q, k_new, v_new, cache_kt, cache_vt, page_table, slope_rows):
    n_seq, dec_seq, _ = q.shape
    n_pages = page_table.shape[1]
    pages_per_step = _BLOCKS_PER_STEP * _PAGES_PER_BLOCK
    assert n_pages % pages_per_step == 0 and dec_seq % SUBLANES == 0
    n_past = n_pages // _PAGES_PER_BLOCK
    past_len = n_pages * PAGE_SIZE
    rows = N_HEADS * dec_seq
    assert n_past % SUBLANES == 0 and n_past <= LANES and dec_seq <= LANES and rows % LANES == 0
    new_spec = pl.BlockSpec((1, dec_seq, D_MODEL), lambda b, jj, pt: (b, 0, 0))

    def page_spec(which):
        return pl.BlockSpec((1, D_MODEL, PAGE_SIZE), lambda b, jj, pt: (pt[b, pages_per_step * jj + which], 0, 0))

    page_specs = [page_spec(w) for w in range(pages_per_step)]
    stats = pltpu.VMEM((rows, LANES), F32)
    grid_spec = pltpu.PrefetchScalarGridSpec(
        num_scalar_prefetch=1,
        grid=(n_seq, n_pages // pages_per_step),
        in_specs=[new_spec, pl.BlockSpec((rows, 1), lambda b, jj, pt: (0, 0))] + page_specs + page_specs
                 + [new_spec, new_spec],
        out_specs=new_spec,
        scratch_shapes=[pltpu.VMEM((D_MODEL, rows), F32), pltpu.VMEM((rows, D_MODEL), BF16),
                        pltpu.VMEM((n_past, rows), F32), stats, stats,
                        pltpu.VMEM((n_past, rows, D_MODEL), F32)],
    )
    return pl.pallas_call(
        functools.partial(_attend_sample_body, dec_seq, n_past, past_len),
        grid_spec=grid_spec,
        out_shape=jax.ShapeDtypeStruct((n_seq, dec_seq, D_MODEL), F32),
        compiler_params=_params("arbitrary", "arbitrary"),
        name="moba_attend_sample",
    )(page_table, q, slope_rows, *([cache_kt] * pages_per_step), *([cache_vt] * pages_per_step), k_new, v_new)


def _out_body(o_ref, sz2_ref, x1_ref, w_ref, g_ref, y_ref):
    y_ref[...] = _out_rows(o_ref[...], sz2_ref[...], x1_ref[...], w_ref, g_ref[...])


def _out(o, sz2, x1, w_out_bf, final_g, rows_per_step):
    n_rows = o.shape[0]
    row_spec = pl.BlockSpec((rows_per_step, D_MODEL), lambda i: (i, 0))
    return pl.pallas_call(
        _out_body,
        grid=(n_rows // rows_per_step,),
        in_specs=[row_spec, row_spec, row_spec, _const_spec((D_MODEL, D_MODEL)), _const_spec((1, D_MODEL))],
        out_specs=row_spec,
        out_shape=jax.ShapeDtypeStruct((n_rows, D_MODEL), F32),
        compiler_params=_params("arbitrary"),
        name="out",
    )(o, sz2, x1, w_out_bf, final_g)


_ROWS_PER_STEP = 256
_OUT_ROWS_PER_STEP = 512


def kernel(x_prompt, x_sample, state_conv, cache_k, cache_v, page_table, a_norm_g, a_w_in, a_conv_w, a_conv_b,
           a_ln_g, a_ln_b, a_w_out, kv_norm_g, w_kv, b_norm_g, b_w_in, b_w_out, final_norm_g):
    n_seq, seq, _ = x_prompt.shape
    n_dec, dec_seq, _ = x_sample.shape
    assert a_w_in.shape[0] == 1 and b_w_in.shape[0] == 1, "one conv layer and one attention layer"
    assert seq % MOBA_BLOCK == 0 and seq >= HIST

    vec = lambda a: a.reshape(1, D_MODEL)
    a_g, cb, ln_g, ln_b = vec(a_norm_g[0]), vec(a_conv_b[0]), vec(a_ln_g[0]), vec(a_ln_b[0])
    kv_g, b_g, fin_g = vec(kv_norm_g), vec(b_norm_g[0]), vec(final_norm_g)
    cw = a_conv_w[0]
    w_in_bf, w_out_bf = a_w_in[0].astype(BF16), a_w_out[0].astype(BF16)
    w_kv_bf, w_kvt_bf = w_kv.astype(BF16), w_kv.T.astype(BF16)
    bw_q_bf, bw_z_bf = b_w_in[0, :, :D_MODEL].astype(BF16), b_w_in[0, :, D_MODEL:].astype(BF16)
    bw_out_bf = b_w_out[0].astype(BF16)
    slopes = jnp.exp2(-8.0 * jnp.arange(1, N_HEADS + 1, dtype=F32) / N_HEADS)

    xp = x_prompt.reshape(n_seq * seq, D_MODEL)
    glu_p, sz_p = _inproj(xp, a_g, w_in_bf, _ROWS_PER_STEP)
    glu_p3 = glu_p.reshape(n_seq, seq, D_MODEL)
    x1_p, sz2_p, qt_p, kt_p, vt_p, kb_p, vtb_p, kmt_p = _mid_prompt(
        glu_p3, sz_p.reshape(n_seq, seq, D_MODEL), x_prompt,
        cw, cb, ln_g, ln_b, w_out_bf, kv_g, w_kvt_bf, b_g, bw_z_bf, bw_q_bf.T)
    kmean_p = kmt_p.transpose(0, 2, 1)[:, :seq // MOBA_BLOCK]
    y_p = _attend_prompt(qt_p, kmean_p, kb_p, vtb_p, slopes, sz2_p, x1_p, bw_out_bf, fin_g)

    n_rows = n_dec * dec_seq
    xs = x_sample.transpose(1, 0, 2).reshape(n_rows, D_MODEL)
    glu_s, sz_s = _inproj(xs, a_g, w_in_bf, _ROWS_PER_STEP)
    glu_s3 = glu_s.reshape(dec_seq, n_dec, D_MODEL)
    hist_t = state_conv[0].transpose(1, 0, 2)
    y_s, hist_new = _conv_sample(hist_t, glu_s3, cw, cb)
    x1_s, q_s, sz2_s, k_s, v_s, kt_s, vt_s = _post_sample(
        y_s.reshape(n_rows, D_MODEL), sz_s, xs, ln_g, ln_b, w_out_bf, kv_g, w_kv_bf, w_kvt_bf, b_g,
        bw_z_bf, bw_q_bf, n_dec, _ROWS_PER_STEP)
    by_seq = lambda a: a.reshape(dec_seq, n_dec, D_MODEL).transpose(1, 0, 2)
    slope_rows = jnp.repeat(slopes, dec_seq).reshape(N_HEADS * dec_seq, 1)
    pages = cache_k.shape[0]
    cache_kt = cache_k.transpose(0, 2, 3, 1).reshape(pages, D_MODEL, PAGE_SIZE)
    cache_vt = cache_v.transpose(0, 2, 3, 1).reshape(pages, D_MODEL, PAGE_SIZE)
    o_s = _attend_sample(by_seq(q_s), by_seq(k_s), by_seq(v_s), cache_kt, cache_vt, page_table, slope_rows)
    y_s = _out(o_s.astype(BF16).transpose(1, 0, 2).reshape(n_rows, D_MODEL), sz2_s, x1_s, bw_out_bf, fin_g,
               _OUT_ROWS_PER_STEP)

    heads_p = lambda a: a.reshape(n_seq, N_HEADS, HEAD_DIM, seq).transpose(0, 3, 1, 2)
    heads_s = lambda a: a.reshape(dec_seq, N_HEADS, HEAD_DIM, n_dec).transpose(3, 0, 1, 2)
    return (y_p.reshape(n_seq, seq, D_MODEL), by_seq(y_s),
            glu_p3[:, seq - HIST:][None], hist_new.transpose(1, 0, 2)[None],
            heads_p(kt_p), heads_p(vt_p), heads_s(kt_s), heads_s(vt_s))
```

```python
import functools
import math

import jax
import jax.numpy as jnp
from jax import lax
from jax.experimental import pallas as pl
from jax.experimental.pallas import tpu as pltpu

D_MODEL = 1024
N_HEADS = 16
HEAD_DIM = 64
CONV_WIDTH = 31
HIST = CONV_WIDTH - 1
MOBA_BLOCK = 256
MOBA_TOPK = 3
PAGE_SIZE = 128
NORM_EPS = 1e-6
MASK_VALUE = -1e30

LANES = 128
SUBLANES = 8
BF16_ROWS = 16
VMEM_LIMIT = 56 * 1024 * 1024
LOG2E = math.log2(math.e)

F32 = jnp.float32
BF16 = jnp.bfloat16


def _idiv(x, n):
    assert n & (n - 1) == 0
    return lax.shift_right_logical(x, n.bit_length() - 1)


def _imod(x, n):
    assert n & (n - 1) == 0
    return lax.bitwise_and(x, n - 1)


def _sigmoid(x):
    return 1.0 / (1.0 + jnp.exp(-x))


def _dot(a, b, precision=None):
    return jnp.dot(a, b, preferred_element_type=F32, precision=precision)


def _dot_nt(a, b):
    return lax.dot_general(a, b, (((1,), (1,)), ((), ())), preferred_element_type=F32)


def _params(*semantics):
    return pltpu.CompilerParams(dimension_semantics=semantics,
                                vmem_limit_bytes=VMEM_LIMIT)


def _const_spec(shape):
    zeros = (0,) * len(shape)
    return pl.BlockSpec(shape, lambda *_: zeros, pipeline_mode=pl.Buffered(1))


def _out_rows(o_bf, sz2_bf, x1, w_ref, final_g):
    x2 = x1 + _dot(o_bf * sz2_bf, w_ref[...])
    return x2 * lax.rsqrt(jnp.mean(x2 * x2, axis=-1, keepdims=True) + NORM_EPS) * final_g


def _rank_is_top(scores, valid, n_top):
    n = scores.shape[0]
    row = lax.broadcasted_iota(jnp.int32, scores.shape, 0).astype(F32)
    left = jnp.where(valid, scores, MASK_VALUE)
    picked = jnp.zeros(scores.shape, F32)
    for _ in range(n_top):
        best = jnp.max(left, axis=0, keepdims=True)
        first = jnp.min(jnp.where(left == best, row, float(n)), axis=0, keepdims=True)
        hit = row == first
        picked = jnp.where(hit, 1.0, picked)
        left = jnp.where(hit, -jnp.inf, left)
    return jnp.where(valid, picked, 0.0) > 0.5


def _inproj_body(x_ref, g_ref, w_ref, glu_ref, sz_ref):
    x = x_ref[...]
    h = x * lax.rsqrt(jnp.mean(x * x, axis=-1, keepdims=True) + NORM_EPS) * g_ref[...]
    hb = h.astype(BF16)
    a = _dot(hb, w_ref[:, 0:D_MODEL])
    b = _dot(hb, w_ref[:, D_MODEL:2 * D_MODEL])
    z = _dot(hb, w_ref[:, 2 * D_MODEL:3 * D_MODEL])
    glu_ref[...] = a * _sigmoid(b)
    sz_ref[...] = z * _sigmoid(z)


def _inproj(x, norm_g, w_in_bf, rows_per_step):
    n_rows = x.shape[0]
    row_spec = pl.BlockSpec((rows_per_step, D_MODEL), lambda i: (i, 0))
    return pl.pallas_call(
        _inproj_body,
        grid=(n_rows // rows_per_step,),
        in_specs=[row_spec, _const_spec((1, D_MODEL)), _const_spec((D_MODEL, 3 * D_MODEL))],
        out_specs=[row_spec, row_spec],
        out_shape=[jax.ShapeDtypeStruct((n_rows, D_MODEL), F32)] * 2,
        compiler_params=_params("arbitrary"),
        name="inproj",
    )(x, norm_g, w_in_bf)


def _post_conv(y, sz, x, ln_g, ln_b, w_out_ref, kv_g, b_g, bw_z_ref):
    mu = jnp.mean(y, axis=-1, keepdims=True)
    yc = y - mu
    var = jnp.mean(yc * yc, axis=-1, keepdims=True)
    ln = yc * lax.rsqrt(var + NORM_EPS) * ln_g + ln_b
    u = ln * _sigmoid(ln) * sz
    x1 = x + _dot(u.astype(BF16), w_out_ref[...])
    xn = x1 * lax.rsqrt(jnp.mean(x1 * x1, axis=-1, keepdims=True) + NORM_EPS)
    xb = (xn * b_g).astype(BF16)
    z2 = _dot(xb, bw_z_ref[...])
    return x1, z2 * _sigmoid(z2), (xn * kv_g).astype(BF16), xb


_MID_ROWS = MOBA_BLOCK
_CARRY = 32
_CONV_ROWS = 128
_TAP0 = _CARRY - HIST
_SHIFT_ROWS = _CARRY + _MID_ROWS - SUBLANES


def _mid_prompt_body(glu_ref, sz_ref, x_ref, cw_ref, cb_ref, lng_ref, lnb_ref, wout_ref,
                     kvg_ref, wkvt_ref, bg_ref, bwz_ref, bwqt_ref,
                     x1_ref, sz2_ref, qt_ref, kt_ref, vt_ref, kb_ref, vtb_ref, kmt_ref,
                     full_ref, shift_ref, y_ref):
    t = pl.program_id(1)

    @pl.when(t == 0)
    def _():
        full_ref[0:_CARRY, :] = jnp.zeros((_CARRY, D_MODEL), F32)

    @pl.when(t > 0)
    def _():
        full_ref[0:_CARRY, :] = full_ref[_MID_ROWS:_MID_ROWS + _CARRY, :]

    full_ref[_CARRY:_CARRY + _MID_ROWS, :] = glu_ref[0]

    def lane_group(c, carry):
        off = pl.multiple_of(c * LANES, LANES)
        for s in range(1, SUBLANES):
            shift_ref[s - 1] = full_ref[pl.ds(s, _SHIFT_ROWS), pl.ds(off, LANES)]
        for rc in range(_MID_ROWS // _CONV_ROWS):
            acc = jnp.zeros((_CONV_ROWS, LANES), F32)
            for s in range(SUBLANES):
                taps = [w for w in range(CONV_WIDTH) if (_TAP0 + w) % SUBLANES == s]
                span = (_TAP0 + taps[-1]) // SUBLANES * SUBLANES + _CONV_ROWS
                if s == 0:
                    rows = full_ref[pl.ds(rc * _CONV_ROWS, span), pl.ds(off, LANES)]
                else:
                    rows = shift_ref[s - 1, pl.ds(rc * _CONV_ROWS, span), :]
                for w in taps:
                    a = (_TAP0 + w) // SUBLANES * SUBLANES
                    acc = acc + rows[a:a + _CONV_ROWS] * cw_ref[w:w + 1, pl.ds(off, LANES)]
            y_ref[pl.ds(rc * _CONV_ROWS, _CONV_ROWS), pl.ds(off, LANES)] = acc + cb_ref[:, pl.ds(off, LANES)]
        return carry

    lax.fori_loop(0, D_MODEL // LANES, lane_group, 0)

    x1, sz2, xkv, xb = _post_conv(y_ref[...], sz_ref[0], x_ref[0], lng_ref[...], lnb_ref[...], wout_ref,
                                  kvg_ref[...], bg_ref[...], bwz_ref)
    x1_ref[0] = x1
    sz2_ref[0] = sz2.astype(BF16)
    qt_ref[0] = _dot_nt(bwqt_ref[...], xb)
    kvt = _dot_nt(wkvt_ref[...], xkv)
    kt = kvt[:D_MODEL]
    vt = kvt[D_MODEL:]
    kt_ref[0] = kt
    vt_ref[0] = vt
    vtb_ref[0] = vt.astype(BF16)
    kb_ref[0] = kt.T.astype(BF16)
    mean_col = jnp.sum(kt, axis=1, keepdims=True) * (1.0 / MOBA_BLOCK)
    lane = lax.broadcasted_iota(jnp.int32, (D_MODEL, LANES), 1)

    @pl.when(t == 0)
    def _():
        kmt_ref[0] = jnp.where(lane == 0, mean_col, 0.0)

    @pl.when(t > 0)
    def _():
        kmt_ref[0] = jnp.where(lane == t, mean_col, kmt_ref[0])


def _mid_prompt(glu, sz, x, cw, cb, ln_g, ln_b, w_out_bf, kv_g, w_kvt_bf, b_g, bw_z_bf, bw_qt_bf):
    n_seq, seq, _ = x.shape
    n_tiles = seq // _MID_ROWS
    assert n_tiles <= LANES
    tile = pl.BlockSpec((1, _MID_ROWS, D_MODEL), lambda b, t: (b, t, 0))
    tile_t = pl.BlockSpec((1, D_MODEL, _MID_ROWS), lambda b, t: (b, 0, t))
    vec = _const_spec((1, D_MODEL))
    square = _const_spec((D_MODEL, D_MODEL))
    f32_rows = jax.ShapeDtypeStruct((n_seq, seq, D_MODEL), F32)
    f32_t = jax.ShapeDtypeStruct((n_seq, D_MODEL, seq), F32)
    return pl.pallas_call(
        _mid_prompt_body,
        grid=(n_seq, n_tiles),
        in_specs=[tile, tile, tile, _const_spec((CONV_WIDTH, D_MODEL)), vec, vec, vec,
                  square, vec, _const_spec((2 * D_MODEL, D_MODEL)), vec, square, square],
        out_specs=[tile, tile, tile_t, tile_t, tile_t, tile, tile_t,
                   pl.BlockSpec((1, D_MODEL, LANES), lambda b, t: (b, 0, 0))],
        out_shape=[f32_rows, jax.ShapeDtypeStruct((n_seq, seq, D_MODEL), BF16), f32_t, f32_t, f32_t,
                   jax.ShapeDtypeStruct((n_seq, seq, D_MODEL), BF16),
                   jax.ShapeDtypeStruct((n_seq, D_MODEL, seq), BF16),
                   jax.ShapeDtypeStruct((n_seq, D_MODEL, LANES), F32)],
        scratch_shapes=[pltpu.VMEM((_CARRY + _MID_ROWS, D_MODEL), F32),
                        pltpu.VMEM((SUBLANES - 1, _SHIFT_ROWS, LANES), F32),
                        pltpu.VMEM((_MID_ROWS, D_MODEL), F32)],
        compiler_params=_params("arbitrary", "arbitrary"),
        name="mid_prompt",
    )(glu, sz, x, cw, cb, ln_g, ln_b, w_out_bf, kv_g, w_kvt_bf, b_g, bw_z_bf, bw_qt_bf)


def _conv_sample_body(dec_seq, hist_ref, glu_ref, cw_ref, cb_ref, y_ref, hist_out_ref):
    def tap(r):
        return hist_ref[r] if r < HIST else glu_ref[r - HIST]

    for t in range(dec_seq):
        acc = tap(t) * cw_ref[0:1, :]
        for w in range(1, CONV_WIDTH):
            acc = acc + tap(t + w) * cw_ref[w:w + 1, :]
        y_ref[t] = acc + cb_ref[...]
    for r in range(HIST):
        hist_out_ref[r] = tap(r + dec_seq)


def _conv_sample(hist_t, glu_t, cw, cb):
    dec_seq, n_seq, _ = glu_t.shape
    lane_blk = lambda rows: pl.BlockSpec((rows, n_seq, LANES), lambda c: (0, 0, c))
    return pl.pallas_call(
        functools.partial(_conv_sample_body, dec_seq),
        grid=(D_MODEL // LANES,),
        in_specs=[lane_blk(HIST), lane_blk(dec_seq),
                  pl.BlockSpec((CONV_WIDTH, LANES), lambda c: (0, c)),
                  pl.BlockSpec((1, LANES), lambda c: (0, c))],
        out_specs=[lane_blk(dec_seq), lane_blk(HIST)],
        out_shape=[jax.ShapeDtypeStruct((dec_seq, n_seq, D_MODEL), F32),
                   jax.ShapeDtypeStruct((HIST, n_seq, D_MODEL), F32)],
        compiler_params=_params("arbitrary"),
        name="conv_sample",
    )(hist_t, glu_t, cw, cb)


def _post_sample_body(t_per_step, n_seq, y_ref, sz_ref, x_ref, lng_ref, lnb_ref, wout_ref,
                      kvg_ref, wkv_ref, wkvt_ref, bg_ref, bwz_ref, bwq_ref,
                      x1_ref, q_ref, sz2_ref, k_ref, v_ref, kt_ref, vt_ref):
    x1, sz2, xkv, xb = _post_conv(y_ref[...], sz_ref[...], x_ref[...], lng_ref[...], lnb_ref[...], wout_ref,
                                  kvg_ref[...], bg_ref[...], bwz_ref)
    x1_ref[...] = x1
    sz2_ref[...] = sz2.astype(BF16)
    q_ref[...] = _dot(xb, bwq_ref[...])
    kv = _dot(xkv, wkv_ref[...])
    k_ref[...] = kv[:, :D_MODEL]
    v_ref[...] = kv[:, D_MODEL:]
    kvt = _dot_nt(wkvt_ref[...], xkv)
    for i in range(t_per_step):
        kt_ref[i] = kvt[:D_MODEL, i * n_seq:(i + 1) * n_seq]
        vt_ref[i] = kvt[D_MODEL:, i * n_seq:(i + 1) * n_seq]


def _post_sample(y, sz, x, ln_g, ln_b, w_out_bf, kv_g, w_kv_bf, w_kvt_bf, b_g, bw_z_bf, bw_q_bf,
                 n_seq, rows_per_step):
    n_rows = y.shape[0]
    dec_seq = n_rows // n_seq
    t_per_step = rows_per_step // n_seq
    assert n_seq % LANES == 0 and rows_per_step % n_seq == 0
    row_spec = pl.BlockSpec((rows_per_step, D_MODEL), lambda i: (i, 0))
    t_spec = pl.BlockSpec((t_per_step, D_MODEL, n_seq), lambda i: (i, 0, 0))
    vec = _const_spec((1, D_MODEL))
    square = _const_spec((D_MODEL, D_MODEL))
    f32_rows = jax.ShapeDtypeStruct((n_rows, D_MODEL), F32)
    f32_t = jax.ShapeDtypeStruct((dec_seq, D_MODEL, n_seq), F32)
    return pl.pallas_call(
        functools.partial(_post_sample_body, t_per_step, n_seq),
        grid=(n_rows // rows_per_step,),
        in_specs=[row_spec, row_spec, row_spec, vec, vec, square, vec,
                  _const_spec((D_MODEL, 2 * D_MODEL)), _const_spec((2 * D_MODEL, D_MODEL)),
                  vec, square, square],
        out_specs=[row_spec] * 5 + [t_spec] * 2,
        out_shape=[f32_rows, f32_rows, jax.ShapeDtypeStruct((n_rows, D_MODEL), BF16), f32_rows, f32_rows]
                  + [f32_t] * 2,
        compiler_params=_params("arbitrary"),
        name="post_sample",
    )(y, sz, x, ln_g, ln_b, w_out_bf, kv_g, w_kv_bf, w_kvt_bf, b_g, bw_z_bf, bw_q_bf)


_HEADS_PER_STEP = 16
_KEY_ROWS = 128
_PV_ROWS = HEAD_DIM + BF16_ROWS


def _softmax_probs(t, shift, m_ref, h, first):
    m_cur = jnp.max(t, axis=0, keepdims=True) - shift
    if first:
        m_new, alpha = m_cur, None
    else:
        m_prev = m_ref[h, 0:1, :]
        m_new = jnp.maximum(m_prev, m_cur)
        alpha = jnp.exp2(m_prev - m_new)
    m_ref[h] = jnp.broadcast_to(m_new, (SUBLANES, t.shape[1]))
    return jnp.exp2(t - (m_new + shift)).astype(BF16), alpha


def _attend_prompt_body(n_blocks, slopes_ref, qt_ref, km_ref, k_ref, vt_ref, sz2_ref, x1_ref, wout_ref, fing_ref,
                        y_ref, qaug_ref, m_ref, acc_ref, o_scr):
    i = pl.program_id(1)
    heads = range(_HEADS_PER_STEP)
    key = lax.broadcasted_iota(jnp.int32, (MOBA_BLOCK, MOBA_BLOCK), 0)
    qry = lax.broadcasted_iota(jnp.int32, (MOBA_BLOCK, MOBA_BLOCK), 1)
    blk = lax.broadcasted_iota(jnp.int32, (n_blocks, MOBA_BLOCK), 0)
    feat_head = _idiv(lax.broadcasted_iota(jnp.int32, (LANES, MOBA_BLOCK), 0), HEAD_DIM)
    lane_head = _idiv(lax.broadcasted_iota(jnp.int32, (n_blocks, LANES), 1), HEAD_DIM)
    own0 = pl.multiple_of(i * MOBA_BLOCK, MOBA_BLOCK)
    slope2 = [slopes_ref[h] * LOG2E for h in heads]
    aug_pad = jnp.zeros((LANES - n_blocks - SUBLANES, MOBA_BLOCK), F32)
    piece_row = lax.broadcasted_iota(jnp.int32, (SUBLANES, MOBA_BLOCK), 0)
    lane = lax.broadcasted_iota(jnp.int32, (_KEY_ROWS, LANES), 1)
    key_row = lax.broadcasted_iota(jnp.int32, (_KEY_ROWS, LANES), 0)
    pos_cols = []
    for c in range(MOBA_BLOCK // _KEY_ROWS):
        pos = (key_row + c * _KEY_ROWS).astype(F32)
        pos_cols.append(jnp.where(lane == n_blocks, pos, jnp.where(lane == n_blocks + 1, pos,
                                  jnp.where(lane == n_blocks + 2, pos, 0.0))))

    def slope_rows(h):
        full = jnp.full((SUBLANES, MOBA_BLOCK), slope2[h], F32)
        hi = full.astype(BF16).astype(F32)
        mid = (full - hi).astype(BF16).astype(F32)
        lo = (full - hi - mid).astype(BF16).astype(F32)
        return jnp.where(piece_row == 0, hi, jnp.where(piece_row == 1, mid, jnp.where(piece_row == 2, lo, 0.0)))

    def pair_rows(h):
        return slice((h // 2) * LANES, (h // 2 + 1) * LANES)

    def v_aug(h, off, width=MOBA_BLOCK):
        return jnp.concatenate([vt_ref[0, h * HEAD_DIM:(h + 1) * HEAD_DIM, pl.ds(off, width)],
                                jnp.ones((BF16_ROWS, width), BF16)], axis=0)

    pair_scores = []
    for pair in range(_HEADS_PER_STEP // 2):
        km_pair = km_ref[0, :, pair_rows(2 * pair)]
        km_heads = jnp.concatenate([jnp.where(lane_head == e, km_pair, 0.0) for e in range(2)], axis=0)
        pair_scores.append(_dot(km_heads, qt_ref[0, pair_rows(2 * pair), :], precision=lax.Precision.HIGHEST))
    for h in heads:
        scores_h = pair_scores[h // 2][(h % 2) * n_blocks:(h % 2 + 1) * n_blocks]
        chosen = _rank_is_top(scores_h, blk < i, MOBA_TOPK)
        bias = jnp.where(chosen, 0.0, MASK_VALUE)
        q_h = jnp.where(feat_head == h % 2, qt_ref[0, pair_rows(h), :] * (HEAD_DIM ** -0.5 * LOG2E), 0.0)
        qaug_ref[h] = jnp.concatenate([q_h, bias, slope_rows(h), aug_pad], axis=0).astype(BF16)

    own_cols = jnp.concatenate(pos_cols, axis=0).astype(BF16)
    s_own = [_dot(jnp.concatenate([k_ref[0, pl.ds(own0, MOBA_BLOCK), pair_rows(h)], own_cols], axis=1), qaug_ref[h])
             for h in heads]
    p_own = [_softmax_probs(jnp.where(qry >= key, s_own[h], MASK_VALUE), 0.0, m_ref, h, first=True)[0]
             for h in heads]
    for h in heads:
        acc_ref[h] = _dot(v_aug(h, own0), p_own[h])

    def past_block(j, carry):
        off = pl.multiple_of(j * MOBA_BLOCK, MOBA_BLOCK)
        cols = [jnp.where(lane == j, 1.0, pos).astype(BF16) for pos in pos_cols]
        gap = ((i - j) * MOBA_BLOCK).astype(F32)
        parts = [(h, c) for h in heads for c in range(MOBA_BLOCK // _KEY_ROWS)]
        s = [_dot(jnp.concatenate([k_ref[0, pl.ds(off + c * _KEY_ROWS, _KEY_ROWS), pair_rows(h)], cols[c]], axis=1),
                  qaug_ref[h]) for h, c in parts]
        probs = [_softmax_probs(s[n], slope2[h] * gap, m_ref, h, first=False) for n, (h, c) in enumerate(parts)]
        for n, (h, c) in enumerate(parts):
            p, alpha = probs[n]
            acc_ref[h] = alpha * acc_ref[h] + _dot(v_aug(h, off + c * _KEY_ROWS, _KEY_ROWS), p)
        return carry

    lax.fori_loop(0, i, past_block, 0)

    for pair in range(_HEADS_PER_STEP // 2):
        outs = []
        for e in range(2):
            acc = acc_ref[2 * pair + e]
            outs.append(acc[:HEAD_DIM] / acc[HEAD_DIM:HEAD_DIM + 1])
        o_scr[:, pair * LANES:(pair + 1) * LANES] = jnp.concatenate(outs, axis=0).T.astype(BF16)

    y_ref[0] = _out_rows(o_scr[...], sz2_ref[0], x1_ref[0], wout_ref, fing_ref[...])


def _attend_prompt(qt, kmean, k_bf, vt_bf, slopes, sz2, x1, w_out_bf, final_g):
    n_seq, _, seq = qt.shape
    n_blocks = seq // MOBA_BLOCK
    assert n_blocks <= LANES and n_blocks % SUBLANES == 0 and 2 * HEAD_DIM == LANES
    assert _HEADS_PER_STEP == N_HEADS, "the fused out stage needs every head of a query block in one step"
    rows = pl.BlockSpec((1, MOBA_BLOCK, D_MODEL), lambda b, i: (b, i, 0))
    whole = lambda shape: pl.BlockSpec(shape, lambda b, i: (b, 0, 0), pipeline_mode=pl.Buffered(1))
    return pl.pallas_call(
        functools.partial(_attend_prompt_body, n_blocks),
        grid=(n_seq, n_blocks),
        in_specs=[pl.BlockSpec(memory_space=pltpu.SMEM),
                  pl.BlockSpec((1, D_MODEL, MOBA_BLOCK), lambda b, i: (b, 0, i)),
                  whole((1, n_blocks, D_MODEL)), whole((1, seq, D_MODEL)), whole((1, D_MODEL, seq)),
                  rows, rows, _const_spec((D_MODEL, D_MODEL)), _const_spec((1, D_MODEL))],
        out_specs=rows,
        out_shape=jax.ShapeDtypeStruct((n_seq, seq, D_MODEL), F32),
        scratch_shapes=[pltpu.VMEM((_HEADS_PER_STEP, 2 * LANES, MOBA_BLOCK), BF16),
                        pltpu.VMEM((_HEADS_PER_STEP, SUBLANES, MOBA_BLOCK), F32),
                        pltpu.VMEM((_HEADS_PER_STEP, _PV_ROWS, MOBA_BLOCK), F32),
                        pltpu.VMEM((MOBA_BLOCK, D_MODEL), BF16)],
        compiler_params=_params("arbitrary", "arbitrary"),
        name="moba_attend_prompt",
    )(slopes, qt, kmean, k_bf, vt_bf, sz2, x1, w_out_bf, final_g)


_BLOCKS_PER_STEP = 8
_PAGES_PER_BLOCK = MOBA_BLOCK // PAGE_SIZE


def _attend_sample_body(dec_seq, n_past, past_len, pt_ref, q_ref, slope_ref, *refs):
    n_pg = _BLOCKS_PER_STEP * _PAGES_PER_BLOCK
    k_pages, v_pages = refs[:n_pg], refs[n_pg:2 * n_pg]
    kn_ref, vn_ref, o_ref, qdt_ref, qdb_ref, sc_ref, m_ref, l_ref, acc_ref = refs[2 * n_pg:]
    jj = pl.program_id(1)
    rows = N_HEADS * dec_seq
    lane = lax.broadcasted_iota(jnp.int32, (rows, LANES), 1)
    slope = slope_ref[...]

    @pl.when(jj == 0)
    def _():
        q = q_ref[0]
        q_rows = jnp.concatenate([q] * N_HEADS, axis=0)
        r_head = _idiv(lax.broadcasted_iota(jnp.int32, (rows, D_MODEL), 0), dec_seq)
        l_head = _idiv(lax.broadcasted_iota(jnp.int32, (rows, D_MODEL), 1), HEAD_DIM)
        q_diag = jnp.where(r_head == l_head, q_rows, 0.0)
        qdt_ref[...] = q_diag.T
        qdb_ref[...] = (q_diag * (HEAD_DIM ** -0.5)).astype(BF16)
        m_ref[...] = jnp.zeros((rows, LANES), F32)
        l_ref[...] = jnp.zeros((rows, LANES), F32)

    q_off = _imod(lax.broadcasted_iota(jnp.int32, (rows, MOBA_BLOCK), 0), dec_seq)
    k_off = lax.broadcasted_iota(jnp.int32, (rows, MOBA_BLOCK), 1)
    blocks = range(_BLOCKS_PER_STEP)
    js = [jj * _BLOCKS_PER_STEP + b for b in blocks]

    def block_of(page_refs, b):
        return jnp.concatenate([r[0] for r in page_refs[b * _PAGES_PER_BLOCK:(b + 1) * _PAGES_PER_BLOCK]], axis=1)

    s = []
    for b in blocks:
        kt_blk = block_of(k_pages, b)
        s.append(_dot(qdb_ref[...], kt_blk.astype(BF16)))
        k_mean = jnp.sum(kt_blk, axis=1, keepdims=True) * (1.0 / MOBA_BLOCK)
        sc_ref[pl.ds(js[b], 1), :] = jnp.sum(qdt_ref[...] * k_mean, axis=0, keepdims=True)
    p = []
    m_all, l_all = m_ref[...], l_ref[...]
    for b in blocks:
        t = s[b] - slope * (q_off - k_off + (past_len - js[b] * MOBA_BLOCK)).astype(F32)
        m_j = jnp.max(t, axis=1, keepdims=True)
        e = jnp.exp(t - m_j)
        m_all = jnp.where(lane == js[b], m_j, m_all)
        l_all = jnp.where(lane == js[b], jnp.sum(e, axis=1, keepdims=True), l_all)
        p.append(e.astype(BF16))
    m_ref[...] = m_all
    l_ref[...] = l_all
    for b in blocks:
        acc_ref[js[b]] = _dot_nt(p[b], block_of(v_pages, b).astype(BF16))

    @pl.when(jj == n_past // _BLOCKS_PER_STEP - 1)
    def _():
        pad = jnp.zeros((LANES - dec_seq, D_MODEL), F32)
        k_new = jnp.concatenate([kn_ref[0], pad], axis=0).astype(BF16)
        v_new = jnp.concatenate([vn_ref[0], pad], axis=0).astype(BF16)
        s_own = _dot_nt(qdb_ref[...], k_new)
        d_own = _imod(lax.broadcasted_iota(jnp.int32, (rows, LANES), 0), dec_seq) - lane
        t_own = jnp.where(d_own >= 0, s_own - slope * d_own.astype(F32), MASK_VALUE)
        m_own = jnp.max(t_own, axis=1, keepdims=True)
        p_own = jnp.exp(t_own - m_own)
        l_own = jnp.sum(p_own, axis=1, keepdims=True)
        acc_own = _dot(p_own.astype(BF16), v_new)

        always = lax.broadcasted_iota(jnp.int32, (n_past, rows), 0) >= 0
        chosen_t = jnp.where(_rank_is_top(sc_ref[...], always, min(MOBA_TOPK, n_past)), 1.0, 0.0)
        chosen = jnp.concatenate([chosen_t, jnp.zeros((LANES - n_past, rows), F32)], axis=0).T > 0.5

        m_all = m_ref[...]
        m_top = jnp.maximum(jnp.max(jnp.where(chosen, m_all, MASK_VALUE), axis=1, keepdims=True), m_own)
        w_all = jnp.where(chosen, jnp.exp(m_all - m_top), 0.0)
        w_own = jnp.exp(m_own - m_top)
        denom = jnp.sum(w_all * l_ref[...], axis=1, keepdims=True) + w_own * l_own
        num = w_own * acc_own
        for b in range(n_past):
            num = num + w_all[:, b:b + 1] * acc_ref[b]
        out = num / denom
        l_head = _idiv(lax.broadcasted_iota(jnp.int32, (dec_seq, D_MODEL), 1), HEAD_DIM)
        res = jnp.zeros((dec_seq, D_MODEL), F32)
        for h in range(N_HEADS):
            res = jnp.where(l_head == h, out[h * dec_seq:(h + 1) * dec_seq, :], res)
        o_ref[0] = res


def _attend_sample(q, k_new, v_new, cache_kt, cache_vt, page_table, slope_rows):
    n_seq, dec_seq, _ = q.shape
    n_pages = page_table.shape[1]
    pages_per_step = _BLOCKS_PER_STEP * _PAGES_PER_BLOCK
    assert n_pages % pages_per_step == 0 and dec_seq % SUBLANES == 0
    n_past = n_pages // _PAGES_PER_BLOCK
    past_len = n_pages * PAGE_SIZE
    rows = N_HEADS * dec_seq
    assert n_past % SUBLANES == 0 and n_past <= LANES and dec_seq <= LANES and rows % LANES == 0
    new_spec = pl.BlockSpec((1, dec_seq, D_MODEL), lambda b, jj, pt: (b, 0, 0))

    def page_spec(which):
        return pl.BlockSpec((1, D_MODEL, PAGE_SIZE), lambda b, jj, pt: (pt[b, pages_per_step * jj + which], 0, 0))

    page_specs = [page_spec(w) for w in range(pages_per_step)]
    stats = pltpu.VMEM((rows, LANES), F32)
    grid_spec = pltpu.PrefetchScalarGridSpec(
        num_scalar_prefetch=1,
        grid=(n_seq, n_pages // pages_per_step),
        in_specs=[new_spec, pl.BlockSpec((rows, 1), lambda b, jj, pt: (0, 0))] + page_specs + page_specs
                 + [new_spec, new_spec],
        out_specs=new_spec,
        scratch_shapes=[pltpu.VMEM((D_MODEL, rows), F32), pltpu.VMEM((rows, D_MODEL), BF16),
                        pltpu.VMEM((n_past, rows), F32), stats, stats,
                        pltpu.VMEM((n_past, rows, D_MODEL), F32)],
    )
    return pl.pallas_call(
        functools.partial(_attend_sample_body, dec_seq, n_past, past_len),
        grid_spec=grid_spec,
        out_shape=jax.ShapeDtypeStruct((n_seq, dec_seq, D_MODEL), F32),
        compiler_params=_params("arbitrary", "arbitrary"),
        name="moba_attend_sample",
    )(page_table, q, slope_rows, *([cache_kt] * pages_per_step), *([cache_vt] * pages_per_step), k_new, v_new)


def _out_body(o_ref, sz2_ref, x1_ref, w_ref, g_ref, y_ref):
    y_ref[...] = _out_rows(o_ref[...], sz2_ref[...], x1_ref[...], w_ref, g_ref[...])


def _out(o, sz2, x1, w_out_bf, final_g, rows_per_step):
    n_rows = o.shape[0]
    row_spec = pl.BlockSpec((rows_per_step, D_MODEL), lambda i: (i, 0))
    return pl.pallas_call(
        _out_body,
        grid=(n_rows // rows_per_step,),
        in_specs=[row_spec, row_spec, row_spec, _const_spec((D_MODEL, D_MODEL)), _const_spec((1, D_MODEL))],
        out_specs=row_spec,
        out_shape=jax.ShapeDtypeStruct((n_rows, D_MODEL), F32),
        compiler_params=_params("arbitrary"),
        name="out",
    )(o, sz2, x1, w_out_bf, final_g)


_ROWS_PER_STEP = 256
_OUT_ROWS_PER_STEP = 512


def kernel(x_prompt, x_sample, state_conv, cache_k, cache_v, page_table, a_norm_g, a_w_in, a_conv_w, a_conv_b,
           a_ln_g, a_ln_b, a_w_out, kv_norm_g, w_kv, b_norm_g, b_w_in, b_w_out, final_norm_g):
    n_seq, seq, _ = x_prompt.shape
    n_dec, dec_seq, _ = x_sample.shape
    assert a_w_in.shape[0] == 1 and b_w_in.shape[0] == 1, "one conv layer and one attention layer"
    assert seq % MOBA_BLOCK == 0 and seq >= HIST

    vec = lambda a: a.reshape(1, D_MODEL)
    a_g, cb, ln_g, ln_b = vec(a_norm_g[0]), vec(a_conv_b[0]), vec(a_ln_g[0]), vec(a_ln_b[0])
    kv_g, b_g, fin_g = vec(kv_norm_g), vec(b_norm_g[0]), vec(final_norm_g)
    cw = a_conv_w[0]
    w_in_bf, w_out_bf = a_w_in[0].astype(BF16), a_w_out[0].astype(BF16)
    w_kv_bf, w_kvt_bf = w_kv.astype(BF16), w_kv.T.astype(BF16)
    bw_q_bf, bw_z_bf = b_w_in[0, :, :D_MODEL].astype(BF16), b_w_in[0, :, D_MODEL:].astype(BF16)
    bw_out_bf = b_w_out[0].astype(BF16)
    slopes = jnp.exp2(-8.0 * jnp.arange(1, N_HEADS + 1, dtype=F32) / N_HEADS)

    xp = x_prompt.reshape(n_seq * seq, D_MODEL)
    glu_p, sz_p = _inproj(xp, a_g, w_in_bf, _ROWS_PER_STEP)
    glu_p3 = glu_p.reshape(n_seq, seq, D_MODEL)
    x1_p, sz2_p, qt_p, kt_p, vt_p, kb_p, vtb_p, kmt_p = _mid_prompt(
        glu_p3, sz_p.reshape(n_seq, seq, D_MODEL), x_prompt,
        cw, cb, ln_g, ln_b, w_out_bf, kv_g, w_kvt_bf, b_g, bw_z_bf, bw_q_bf.T)
    kmean_p = kmt_p.transpose(0, 2, 1)[:, :seq // MOBA_BLOCK]
    y_p = _attend_prompt(qt_p, kmean_p, kb_p, vtb_p, slopes, sz2_p, x1_p, bw_out_bf, fin_g)

    n_rows = n_dec * dec_seq
    xs = x_sample.transpose(1, 0, 2).reshape(n_rows, D_MODEL)
    glu_s, sz_s = _inproj(xs, a_g, w_in_bf, _ROWS_PER_STEP)
    glu_s3 = glu_s.reshape(dec_seq, n_dec, D_MODEL)
    hist_t = state_conv[0].transpose(1, 0, 2)
    y_s, hist_new = _conv_sample(hist_t, glu_s3, cw, cb)
    x1_s, q_s, sz2_s, k_s, v_s, kt_s, vt_s = _post_sample(
        y_s.reshape(n_rows, D_MODEL), sz_s, xs, ln_g, ln_b, w_out_bf, kv_g, w_kv_bf, w_kvt_bf, b_g,
        bw_z_bf, bw_q_bf, n_dec, _ROWS_PER_STEP)
    by_seq = lambda a: a.reshape(dec_seq, n_dec, D_MODEL).transpose(1, 0, 2)
    slope_rows = jnp.repeat(slopes, dec_seq).reshape(N_HEADS * dec_seq, 1)
    pages = cache_k.shape[0]
    cache_kt = cache_k.transpose(0, 2, 3, 1).reshape(pages, D_MODEL, PAGE_SIZE)
    cache_vt = cache_v.transpose(0, 2, 3, 1).reshape(pages, D_MODEL, PAGE_SIZE)
    o_s = _attend_sample(by_seq(q_s), by_seq(k_s), by_seq(v_s), cache_kt, cache_vt, page_table, slope_rows)
    y_s = _out(o_s.astype(BF16).transpose(1, 0, 2).reshape(n_rows, D_MODEL), sz2_s, x1_s, bw_out_bf, fin_g,
               _OUT_ROWS_PER_STEP)

    heads_p = lambda a: a.reshape(n_seq, N_HEADS, HEAD_DIM, seq).transpose(0, 3, 1, 2)
    heads_s = lambda a: a.reshape(dec_seq, N_HEADS, HEAD_DIM, n_dec).transpose(3, 0, 1, 2)
    return (y_p.reshape(n_seq, seq, D_MODEL), by_seq(y_s),
            glu_p3[:, seq - HIST:][None], hist_new.transpose(1, 0, 2)[None],
            heads_p(kt_p), heads_p(vt_p), heads_s(kt_s), heads_s(vt_s))
```

```python
import functools
import math

import jax
import jax.numpy as jnp
from jax import lax
from jax.experimental import pallas as pl
from jax.experimental.pallas import tpu as pltpu

D_MODEL = 1024
N_HEADS = 16
HEAD_DIM = 64
CONV_WIDTH = 31
HIST = CONV_WIDTH - 1
MOBA_BLOCK = 256
MOBA_TOPK = 3
PAGE_SIZE = 128
NORM_EPS = 1e-6
MASK_VALUE = -1e30

LANES = 128
SUBLANES = 8
BF16_ROWS = 16
VMEM_LIMIT = 56 * 1024 * 1024
LOG2E = math.log2(math.e)

F32 = jnp.float32
BF16 = jnp.bfloat16


def _idiv(x, n):
    assert n & (n - 1) == 0
    return lax.shift_right_logical(x, n.bit_length() - 1)


def _imod(x, n):
    assert n & (n - 1) == 0
    return lax.bitwise_and(x, n - 1)


def _sigmoid(x):
    return 1.0 / (1.0 + jnp.exp(-x))


def _dot(a, b, precision=None):
    return jnp.dot(a, b, preferred_element_type=F32, precision=precision)


def _dot_nt(a, b):
    return lax.dot_general(a, b, (((1,), (1,)), ((), ())), preferred_element_type=F32)


def _params(*semantics):
    return pltpu.CompilerParams(dimension_semantics=semantics,
                                vmem_limit_bytes=VMEM_LIMIT)


def _const_spec(shape):
    zeros = (0,) * len(shape)
    return pl.BlockSpec(shape, lambda *_: zeros, pipeline_mode=pl.Buffered(1))


def _out_rows(o_bf, sz2_bf, x1, w_ref, final_g):
    x2 = x1 + _dot(o_bf * sz2_bf, w_ref[...])
    return x2 * lax.rsqrt(jnp.mean(x2 * x2, axis=-1, keepdims=True) + NORM_EPS) * final_g


def _rank_is_top(scores, valid, n_top):
    n = scores.shape[0]
    row = lax.broadcasted_iota(jnp.int32, scores.shape, 0).astype(F32)
    left = jnp.where(valid, scores, MASK_VALUE)
    picked = jnp.zeros(scores.shape, F32)
    for _ in range(n_top):
        best = jnp.max(left, axis=0, keepdims=True)
        first = jnp.min(jnp.where(left == best, row, float(n)), axis=0, keepdims=True)
        hit = row == first
        picked = jnp.where(hit, 1.0, picked)
        left = jnp.where(hit, -jnp.inf, left)
    return jnp.where(valid, picked, 0.0) > 0.5


def _inproj_body(x_ref, g_ref, w_ref, glu_ref, sz_ref):
    x = x_ref[...]
    h = x * lax.rsqrt(jnp.mean(x * x, axis=-1, keepdims=True) + NORM_EPS) * g_ref[...]
    hb = h.astype(BF16)
    a = _dot(hb, w_ref[:, 0:D_MODEL])
    b = _dot(hb, w_ref[:, D_MODEL:2 * D_MODEL])
    z = _dot(hb, w_ref[:, 2 * D_MODEL:3 * D_MODEL])
    glu_ref[...] = a * _sigmoid(b)
    sz_ref[...] = z * _sigmoid(z)


def _inproj(x, norm_g, w_in_bf, rows_per_step):
    n_rows = x.shape[0]
    row_spec = pl.BlockSpec((rows_per_step, D_MODEL), lambda i: (i, 0))
    return pl.pallas_call(
        _inproj_body,
        grid=(n_rows // rows_per_step,),
        in_specs=[row_spec, _const_spec((1, D_MODEL)), _const_spec((D_MODEL, 3 * D_MODEL))],
        out_specs=[row_spec, row_spec],
        out_shape=[jax.ShapeDtypeStruct((n_rows, D_MODEL), F32)] * 2,
        compiler_params=_params("arbitrary"),
        name="inproj",
    )(x, norm_g, w_in_bf)


def _post_conv(y, sz, x, ln_g, ln_b, w_out_ref, kv_g, b_g, bw_z_ref):
    mu = jnp.mean(y, axis=-1, keepdims=True)
    yc = y - mu
    var = jnp.mean(yc * yc, axis=-1, keepdims=True)
    ln = yc * lax.rsqrt(var + NORM_EPS) * ln_g + ln_b
    u = ln * _sigmoid(ln) * sz
    x1 = x + _dot(u.astype(BF16), w_out_ref[...])
    xn = x1 * lax.rsqrt(jnp.mean(x1 * x1, axis=-1, keepdims=True) + NORM_EPS)
    xb = (xn * b_g).astype(BF16)
    z2 = _dot(xb, bw_z_ref[...])
    return x1, z2 * _sigmoid(z2), (xn * kv_g).astype(BF16), xb


_MID_ROWS = MOBA_BLOCK
_CARRY = 32
_CONV_ROWS = 128
_TAP0 = _CARRY - HIST
_SHIFT_ROWS = _CARRY + _MID_ROWS - SUBLANES


def _mid_prompt_body(glu_ref, sz_ref, x_ref, cw_ref, cb_ref, lng_ref, lnb_ref, wout_ref,
                     kvg_ref, wkvt_ref, bg_ref, bwz_ref, bwqt_ref,
                     x1_ref, sz2_ref, qt_ref, kt_ref, vt_ref, kb_ref, vtb_ref, kmt_ref,
                     full_ref, shift_ref, y_ref):
    t = pl.program_id(1)

    @pl.when(t == 0)
    def _():
        full_ref[0:_CARRY, :] = jnp.zeros((_CARRY, D_MODEL), F32)

    @pl.when(t > 0)
    def _():
        full_ref[0:_CARRY, :] = full_ref[_MID_ROWS:_MID_ROWS + _CARRY, :]

    full_ref[_CARRY:_CARRY + _MID_ROWS, :] = glu_ref[0]

    def lane_group(c, carry):
        off = pl.multiple_of(c * LANES, LANES)
        for s in range(1, SUBLANES):
            shift_ref[s - 1] = full_ref[pl.ds(s, _SHIFT_ROWS), pl.ds(off, LANES)]
        for rc in range(_MID_ROWS // _CONV_ROWS):
            acc = jnp.zeros((_CONV_ROWS, LANES), F32)
            for s in range(SUBLANES):
                taps = [w for w in range(CONV_WIDTH) if (_TAP0 + w) % SUBLANES == s]
                span = (_TAP0 + taps[-1]) // SUBLANES * SUBLANES + _CONV_ROWS
                if s == 0:
                    rows = full_ref[pl.ds(rc * _CONV_ROWS, span), pl.ds(off, LANES)]
                else:
                    rows = shift_ref[s - 1, pl.ds(rc * _CONV_ROWS, span), :]
                for w in taps:
                    a = (_TAP0 + w) // SUBLANES * SUBLANES
                    acc = acc + rows[a:a + _CONV_ROWS] * cw_ref[w:w + 1, pl.ds(off, LANES)]
            y_ref[pl.ds(rc * _CONV_ROWS, _CONV_ROWS), pl.ds(off, LANES)] = acc + cb_ref[:, pl.ds(off, LANES)]
        return carry

    lax.fori_loop(0, D_MODEL // LANES, lane_group, 0)

    x1, sz2, xkv, xb = _post_conv(y_ref[...], sz_ref[0], x_ref[0], lng_ref[...], lnb_ref[...], wout_ref,
                                  kvg_ref[...], bg_ref[...], bwz_ref)
    x1_ref[0] = x1
    sz2_ref[0] = sz2.astype(BF16)
    qt_ref[0] = _dot_nt(bwqt_ref[...], xb)
    kvt = _dot_nt(wkvt_ref[...], xkv)
    kt = kvt[:D_MODEL]
    vt = kvt[D_MODEL:]
    kt_ref[0] = kt
    vt_ref[0] = vt
    vtb_ref[0] = vt.astype(BF16)
    kb_ref[0] = kt.T.astype(BF16)
    mean_col = jnp.sum(kt, axis=1, keepdims=True) * (1.0 / MOBA_BLOCK)
    lane = lax.broadcasted_iota(jnp.int32, (D_MODEL, LANES), 1)

    @pl.when(t == 0)
    def _():
        kmt_ref[0] = jnp.where(lane == 0, mean_col, 0.0)

    @pl.when(t > 0)
    def _():
        kmt_ref[0] = jnp.where(lane == t, mean_col, kmt_ref[0])


def _mid_prompt(glu, sz, x, cw, cb, ln_g, ln_b, w_out_bf, kv_g, w_kvt_bf, b_g, bw_z_bf, bw_qt_bf):
    n_seq, seq, _ = x.shape
    n_tiles = seq // _MID_ROWS
    assert n_tiles <= LANES
    tile = pl.BlockSpec((1, _MID_ROWS, D_MODEL), lambda b, t: (b, t, 0))
    tile_t = pl.BlockSpec((1, D_MODEL, _MID_ROWS), lambda b, t: (b, 0, t))
    vec = _const_spec((1, D_MODEL))
    square = _const_spec((D_MODEL, D_MODEL))
    f32_rows = jax.ShapeDtypeStruct((n_seq, seq, D_MODEL), F32)
    f32_t = jax.ShapeDtypeStruct((n_seq, D_MODEL, seq), F32)
    return pl.pallas_call(
        _mid_prompt_body,
        grid=(n_seq, n_tiles),
        in_specs=[tile, tile, tile, _const_spec((CONV_WIDTH, D_MODEL)), vec, vec, vec,
                  square, vec, _const_spec((2 * D_MODEL, D_MODEL)), vec, square, square],
        out_specs=[tile, tile, tile_t, tile_t, tile_t, tile, tile_t,
                   pl.BlockSpec((1, D_MODEL, LANES), lambda b, t: (b, 0, 0))],
        out_shape=[f32_rows, jax.ShapeDtypeStruct((n_seq, seq, D_MODEL), BF16), f32_t, f32_t, f32_t,
                   jax.ShapeDtypeStruct((n_seq, seq, D_MODEL), BF16),
                   jax.ShapeDtypeStruct((n_seq, D_MODEL, seq), BF16),
                   jax.ShapeDtypeStruct((n_seq, D_MODEL, LANES), F32)],
        scratch_shapes=[pltpu.VMEM((_CARRY + _MID_ROWS, D_MODEL), F32),
                        pltpu.VMEM((SUBLANES - 1, _SHIFT_ROWS, LANES), F32),
                        pltpu.VMEM((_MID_ROWS, D_MODEL), F32)],
        compiler_params=_params("arbitrary", "arbitrary"),
        name="mid_prompt",
    )(glu, sz, x, cw, cb, ln_g, ln_b, w_out_bf, kv_g, w_kvt_bf, b_g, bw_z_bf, bw_qt_bf)


def _conv_sample_body(dec_seq, hist_ref, glu_ref, cw_ref, cb_ref, y_ref, hist_out_ref):
    def tap(r):
        return hist_ref[r] if r < HIST else glu_ref[r - HIST]

    for t in range(dec_seq):
        acc = tap(t) * cw_ref[0:1, :]
        for w in range(1, CONV_WIDTH):
            acc = acc + tap(t + w) * cw_ref[w:w + 1, :]
        y_ref[t] = acc + cb_ref[...]
    for r in range(HIST):
        hist_out_ref[r] = tap(r + dec_seq)


def _conv_sample(hist_t, glu_t, cw, cb):
    dec_seq, n_seq, _ = glu_t.shape
    lane_blk = lambda rows: pl.BlockSpec((rows, n_seq, LANES), lambda c: (0, 0, c))
    return pl.pallas_call(
        functools.partial(_conv_sample_body, dec_seq),
        grid=(D_MODEL // LANES,),
        in_specs=[lane_blk(HIST), lane_blk(dec_seq),
                  pl.BlockSpec((CONV_WIDTH, LANES), lambda c: (0, c)),
                  pl.BlockSpec((1, LANES), lambda c: (0, c))],
        out_specs=[lane_blk(dec_seq), lane_blk(HIST)],
        out_shape=[jax.ShapeDtypeStruct((dec_seq, n_seq, D_MODEL), F32),
                   jax.ShapeDtypeStruct((HIST, n_seq, D_MODEL), F32)],
        compiler_params=_params("arbitrary"),
        name="conv_sample",
    )(hist_t, glu_t, cw, cb)


def _post_sample_body(t_per_step, n_seq, y_ref, sz_ref, x_ref, lng_ref, lnb_ref, wout_ref,
                      kvg_ref, wkv_ref, wkvt_ref, bg_ref, bwz_ref, bwq_ref,
                      x1_ref, q_ref, sz2_ref, k_ref, v_ref, kt_ref, vt_ref):
    x1, sz2, xkv, xb = _post_conv(y_ref[...], sz_ref[...], x_ref[...], lng_ref[...], lnb_ref[...], wout_ref,
                                  kvg_ref[...], bg_ref[...], bwz_ref)
    x1_ref[...] = x1
    sz2_ref[...] = sz2.astype(BF16)
    q_ref[...] = _dot(xb, bwq_ref[...])
    kv = _dot(xkv, wkv_ref[...])
    k_ref[...] = kv[:, :D_MODEL]
    v_ref[...] = kv[:, D_MODEL:]
    kvt = _dot_nt(wkvt_ref[...], xkv)
    for i in range(t_per_step):
        kt_ref[i] = kvt[:D_MODEL, i * n_seq:(i + 1) * n_seq]
        vt_ref[i] = kvt[D_MODEL:, i * n_seq:(i + 1) * n_seq]


def _post_sample(y, sz, x, ln_g, ln_b, w_out_bf, kv_g, w_kv_bf, w_kvt_bf, b_g, bw_z_bf, bw_q_bf,
                 n_seq, rows_per_step):
    n_rows = y.shape[0]
    dec_seq = n_rows // n_seq
    t_per_step = rows_per_step // n_seq
    assert n_seq % LANES == 0 and rows_per_step % n_seq == 0
    row_spec = pl.BlockSpec((rows_per_step, D_MODEL), lambda i: (i, 0))
    t_spec = pl.BlockSpec((t_per_step, D_MODEL, n_seq), lambda i: (i, 0, 0))
    vec = _const_spec((1, D_MODEL))
    square = _const_spec((D_MODEL, D_MODEL))
    f32_rows = jax.ShapeDtypeStruct((n_rows, D_MODEL), F32)
    f32_t = jax.ShapeDtypeStruct((dec_seq, D_MODEL, n_seq), F32)
    return pl.pallas_call(
        functools.partial(_post_sample_body, t_per_step, n_seq),
        grid=(n_rows // rows_per_step,),
        in_specs=[row_spec, row_spec, row_spec, vec, vec, square, vec,
                  _const_spec((D_MODEL, 2 * D_MODEL)), _const_spec((2 * D_MODEL, D_MODEL)),
                  vec, square, square],
        out_specs=[row_spec] * 5 + [t_spec] * 2,
        out_shape=[f32_rows, f32_rows, jax.ShapeDtypeStruct((n_rows, D_MODEL), BF16), f32_rows, f32_rows]
                  + [f32_t] * 2,
        compiler_params=_params("arbitrary"),
        name="post_sample",
    )(y, sz, x, ln_g, ln_b, w_out_bf, kv_g, w_kv_bf, w_kvt_bf, b_g, bw_z_bf, bw_q_bf)


_HEADS_PER_STEP = 16
_KEY_ROWS = 128
_PV_ROWS = HEAD_DIM + BF16_ROWS


def _softmax_probs(t, shift, m_ref, h, first):
    m_cur = jnp.max(t, axis=0, keepdims=True) - shift
    if first:
        m_new, alpha = m_cur, None
    else:
        m_prev = m_ref[h, 0:1, :]
        m_new = jnp.maximum(m_prev, m_cur)
        alpha = jnp.exp2(m_prev - m_new)
    m_ref[h] = jnp.broadcast_to(m_new, (SUBLANES, t.shape[1]))
    return jnp.exp2(t - (m_new + shift)).astype(BF16), alpha


def _attend_prompt_body(n_blocks, slopes_ref, qt_ref, km_ref, k_ref, vt_ref, sz2_ref, x1_ref, wout_ref, fing_ref,
                        y_ref, qaug_ref, m_ref, acc_ref, o_scr):
    i = pl.program_id(1)
    heads = range(_HEADS_PER_STEP)
    key = lax.broadcasted_iota(jnp.int32, (MOBA_BLOCK, MOBA_BLOCK), 0)
    qry = lax.broadcasted_iota(jnp.int32, (MOBA_BLOCK, MOBA_BLOCK), 1)
    blk = lax.broadcasted_iota(jnp.int32, (n_blocks, MOBA_BLOCK), 0)
    feat_head = _idiv(lax.broadcasted_iota(jnp.int32, (LANES, MOBA_BLOCK), 0), HEAD_DIM)
    lane_head = _idiv(lax.broadcasted_iota(jnp.int32, (n_blocks, LANES), 1), HEAD_DIM)
    own0 = pl.multiple_of(i * MOBA_BLOCK, MOBA_BLOCK)
    slope2 = [slopes_ref[h] * LOG2E for h in heads]
    aug_pad = jnp.zeros((LANES - n_blocks - SUBLANES, MOBA_BLOCK), F32)
    piece_row = lax.broadcasted_iota(jnp.int32, (SUBLANES, MOBA_BLOCK), 0)
    lane = lax.broadcasted_iota(jnp.int32, (_KEY_ROWS, LANES), 1)
    key_row = lax.broadcasted_iota(jnp.int32, (_KEY_ROWS, LANES), 0)
    pos_cols = []
    for c in range(MOBA_BLOCK // _KEY_ROWS):
        pos = (key_row + c * _KEY_ROWS).astype(F32)
        pos_cols.append(jnp.where(lane == n_blocks, pos, jnp.where(lane == n_blocks + 1, pos,
                                  jnp.where(lane == n_blocks + 2, pos, 0.0))))

    def slope_rows(h):
        full = jnp.full((SUBLANES, MOBA_BLOCK), slope2[h], F32)
        hi = full.astype(BF16).astype(F32)
        mid = (full - hi).astype(BF16).astype(F32)
        lo = (full - hi - mid).astype(BF16).astype(F32)
        return jnp.where(piece_row == 0, hi, jnp.where(piece_row == 1, mid, jnp.where(piece_row == 2, lo, 0.0)))

    def pair_rows(h):
        return slice((h // 2) * LANES, (h // 2 + 1) * LANES)

    def v_aug(h, off, width=MOBA_BLOCK):
        return jnp.concatenate([vt_ref[0, h * HEAD_DIM:(h + 1) * HEAD_DIM, pl.ds(off, width)],
                                jnp.ones((BF16_ROWS, width), BF16)], axis=0)

    pair_scores = []
    for pair in range(_HEADS_PER_STEP // 2):
        km_pair = km_ref[0, :, pair_rows(2 * pair)]
        km_heads = jnp.concatenate([jnp.where(lane_head == e, km_pair, 0.0) for e in range(2)], axis=0)
        pair_scores.append(_dot(km_heads, qt_ref[0, pair_rows(2 * pair), :], precision=lax.Precision.HIGHEST))
    for h in heads:
        scores_h = pair_scores[h // 2][(h % 2) * n_blocks:(h % 2 + 1) * n_blocks]
        chosen = _rank_is_top(scores_h, blk < i, MOBA_TOPK)
        bias = jnp.where(chosen, 0.0, MASK_VALUE)
        q_h = jnp.where(feat_head == h % 2, qt_ref[0, pair_rows(h), :] * (HEAD_DIM ** -0.5 * LOG2E), 0.0)
        qaug_ref[h] = jnp.concatenate([q_h, bias, slope_rows(h), aug_pad], axis=0).astype(BF16)

    own_cols = jnp.concatenate(pos_cols, axis=0).astype(BF16)
    s_own = [_dot(jnp.concatenate([k_ref[0, pl.ds(own0, MOBA_BLOCK), pair_rows(h)], own_cols], axis=1), qaug_ref[h])
             for h in heads]
    p_own = [_softmax_probs(jnp.where(qry >= key, s_own[h], MASK_VALUE), 0.0, m_ref, h, first=True)[0]
             for h in heads]
    for h in heads:
        acc_ref[h] = _dot(v_aug(h, own0), p_own[h])

    def past_block(j, carry):
        off = pl.multiple_of(j * MOBA_BLOCK, MOBA_BLOCK)
        cols = [jnp.where(lane == j, 1.0, pos).astype(BF16) for pos in pos_cols]
        gap = ((i - j) * MOBA_BLOCK).astype(F32)
        parts = [(h, c) for h in heads for c in range(MOBA_BLOCK // _KEY_ROWS)]
        s = [_dot(jnp.concatenate([k_ref[0, pl.ds(off + c * _KEY_ROWS, _KEY_ROWS), pair_rows(h)], cols[c]], axis=1),
                  qaug_ref[h]) for h, c in parts]
        probs = [_softmax_probs(s[n], slope2[h] * gap, m_ref, h, first=False) for n, (h, c) in enumerate(parts)]
        for n, (h, c) in enumerate(parts):
            p, alpha = probs[n]
            acc_ref[h] = alpha * acc_ref[h] + _dot(v_aug(h, off + c * _KEY_ROWS, _KEY_ROWS), p)
        return carry

    lax.fori_loop(0, i, past_block, 0)

    for pair in range(_HEADS_PER_STEP // 2):
        outs = []
        for e in range(2):
            acc = acc_ref[2 * pair + e]
            outs.append(acc[:HEAD_DIM] / acc[HEAD_DIM:HEAD_DIM + 1])
        o_scr[:, pair * LANES:(pair + 1) * LANES] = jnp.concatenate(outs, axis=0).T.astype(BF16)

    y_ref[0] = _out_rows(o_scr[...], sz2_ref[0], x1_ref[0], wout_ref, fing_ref[...])


def _attend_prompt(qt, kmean, k_bf, vt_bf, slopes, sz2, x1, w_out_bf, final_g):
    n_seq, _, seq = qt.shape
    n_blocks = seq // MOBA_BLOCK
    assert n_blocks <= LANES and n_blocks % SUBLANES == 0 and 2 * HEAD_DIM == LANES
    assert _HEADS_PER_STEP == N_HEADS, "the fused out stage needs every head of a query block in one step"
    rows = pl.BlockSpec((1, MOBA_BLOCK, D_MODEL), lambda b, i: (b, i, 0))
    whole = lambda shape: pl.BlockSpec(shape, lambda b, i: (b, 0, 0), pipeline_mode=pl.Buffered(1))
    return pl.pallas_call(
        functools.partial(_attend_prompt_body, n_blocks),
        grid=(n_seq, n_blocks),
        in_specs=[pl.BlockSpec(memory_space=pltpu.SMEM),
                  pl.BlockSpec((1, D_MODEL, MOBA_BLOCK), lambda b, i: (b, 0, i)),
                  whole((1, n_blocks, D_MODEL)), whole((1, seq, D_MODEL)), whole((1, D_MODEL, seq)),
                  rows, rows, _const_spec((D_MODEL, D_MODEL)), _const_spec((1, D_MODEL))],
        out_specs=rows,
        out_shape=jax.ShapeDtypeStruct((n_seq, seq, D_MODEL), F32),
        scratch_shapes=[pltpu.VMEM((_HEADS_PER_STEP, 2 * LANES, MOBA_BLOCK), BF16),
                        pltpu.VMEM((_HEADS_PER_STEP, SUBLANES, MOBA_BLOCK), F32),
                        pltpu.VMEM((_HEADS_PER_STEP, _PV_ROWS, MOBA_BLOCK), F32),
                        pltpu.VMEM((MOBA_BLOCK, D_MODEL), BF16)],
        compiler_params=_params("arbitrary", "arbitrary"),
        name="moba_attend_prompt",
    )(slopes, qt, kmean, k_bf, vt_bf, sz2, x1, w_out_bf, final_g)


_BLOCKS_PER_STEP = 8
_PAGES_PER_BLOCK = MOBA_BLOCK // PAGE_SIZE


def _attend_sample_body(dec_seq, n_past, past_len, pt_ref, q_ref, slope_ref, *refs):
    n_pg = _BLOCKS_PER_STEP * _PAGES_PER_BLOCK
    k_pages, v_pages = refs[:n_pg], refs[n_pg:2 * n_pg]
    kn_ref, vn_ref, o_ref, qdt_ref, qdb_ref, sc_ref, m_ref, l_ref, acc_ref = refs[2 * n_pg:]
    jj = pl.program_id(1)
    rows = N_HEADS * dec_seq
    lane = lax.broadcasted_iota(jnp.int32, (rows, LANES), 1)
    slope = slope_ref[...]

    @pl.when(jj == 0)
    def _():
        q = q_ref[0]
        q_rows = jnp.concatenate([q] * N_HEADS, axis=0)
        r_head = _idiv(lax.broadcasted_iota(jnp.int32, (rows, D_MODEL), 0), dec_seq)
        l_head = _idiv(lax.broadcasted_iota(jnp.int32, (rows, D_MODEL), 1), HEAD_DIM)
        q_diag = jnp.where(r_head == l_head, q_rows, 0.0)
        qdt_ref[...] = q_diag.T
        qdb_ref[...] = (q_diag * (HEAD_DIM ** -0.5)).astype(BF16)
        m_ref[...] = jnp.zeros((rows, LANES), F32)
        l_ref[...] = jnp.zeros((rows, LANES), F32)

    q_off = _imod(lax.broadcasted_iota(jnp.int32, (rows, MOBA_BLOCK), 0), dec_seq)
    k_off = lax.broadcasted_iota(jnp.int32, (rows, MOBA_BLOCK), 1)
    blocks = range(_BLOCKS_PER_STEP)
    js = [jj * _BLOCKS_PER_STEP + b for b in blocks]

    def block_of(page_refs, b):
        return jnp.concatenate([r[0] for r in page_refs[b * _PAGES_PER_BLOCK:(b + 1) * _PAGES_PER_BLOCK]], axis=1)

    s = []
    for b in blocks:
        kt_blk = block_of(k_pages, b)
        s.append(_dot(qdb_ref[...], kt_blk.astype(BF16)))
        k_mean = jnp.sum(kt_blk, axis=1, keepdims=True) * (1.0 / MOBA_BLOCK)
        sc_ref[pl.ds(js[b], 1), :] = jnp.sum(qdt_ref[...] * k_mean, axis=0, keepdims=True)
    p = []
    m_all, l_all = m_ref[...], l_ref[...]
    for b in blocks:
        t = s[b] - slope * (q_off - k_off + (past_len - js[b] * MOBA_BLOCK)).astype(F32)
        m_j = jnp.max(t, axis=1, keepdims=True)
        e = jnp.exp(t - m_j)
        m_all = jnp.where(lane == js[b], m_j, m_all)
        l_all = jnp.where(lane == js[b], jnp.sum(e, axis=1, keepdims=True), l_all)
        p.append(e.astype(BF16))
    m_ref[...] = m_all
    l_ref[...] = l_all
    for b in blocks:
        acc_ref[js[b]] = _dot_nt(p[b], block_of(v_pages, b).astype(BF16))

    @pl.when(jj == n_past // _BLOCKS_PER_STEP - 1)
    def _():
        pad = jnp.zeros((LANES - dec_seq, D_MODEL), F32)
        k_new = jnp.concatenate([kn_ref[0], pad], axis=0).astype(BF16)
        v_new = jnp.concatenate([vn_ref[0], pad], axis=0).astype(BF16)
        s_own = _dot_nt(qdb_ref[...], k_new)
        d_own = _imod(lax.broadcasted_iota(jnp.int32, (rows, LANES), 0), dec_seq) - lane
        t_own = jnp.where(d_own >= 0, s_own - slope * d_own.astype(F32), MASK_VALUE)
        m_own = jnp.max(t_own, axis=1, keepdims=True)
        p_own = jnp.exp(t_own - m_own)
        l_own = jnp.sum(p_own, axis=1, keepdims=True)
        acc_own = _dot(p_own.astype(BF16), v_new)

        always = lax.broadcasted_iota(jnp.int32, (n_past, rows), 0) >= 0
        chosen_t = jnp.where(_rank_is_top(sc_ref[...], always, min(MOBA_TOPK, n_past)), 1.0, 0.0)
        chosen = jnp.concatenate([chosen_t, jnp.zeros((LANES - n_past, rows), F32)], axis=0).T > 0.5

        m_all = m_ref[...]
        m_top = jnp.maximum(jnp.max(jnp.where(chosen, m_all, MASK_VALUE), axis=1, keepdims=True), m_own)
        w_all = jnp.where(chosen, jnp.exp(m_all - m_top), 0.0)
        w_own = jnp.exp(m_own - m_top)
        denom = jnp.sum(w_all * l_ref[...], axis=1, keepdims=True) + w_own * l_own
        num = w_own * acc_own
        for b in range(n_past):
            num = num + w_all[:, b:b + 1] * acc_ref[b]
        out = num / denom
        l_head = _idiv(lax.broadcasted_iota(jnp.int32, (dec_seq, D_MODEL), 1), HEAD_DIM)
        res = jnp.zeros((dec_seq, D_MODEL), F32)
        for h in range(N_HEADS):
            res = jnp.where(l_head == h, out[h * dec_seq:(h + 1) * dec_seq, :], res)
        o_ref[0] = res


def _attend_sample(q, k_new, v_new, cache_kt, cache_vt, page_table, slope_rows):
    n_seq, dec_seq, _ = q.shape
    n_pages = page_table.shape[1]
    pages_per_step = _BLOCKS_PER_STEP * _PAGES_PER_BLOCK
    assert n_pages % pages_per_step == 0 and dec_seq % SUBLANES == 0
    n_past = n_pages // _PAGES_PER_BLOCK
    past_len = n_pages * PAGE_SIZE
    rows = N_HEADS * dec_seq
    assert n_past % SUBLANES == 0 and n_past <= LANES and dec_seq <= LANES and rows % LANES == 0
    new_spec = pl.BlockSpec((1, dec_seq, D_MODEL), lambda b, jj, pt: (b, 0, 0))

    def page_spec(which):
        return pl.BlockSpec((1, D_MODEL, PAGE_SIZE), lambda b, jj, pt: (pt[b, pages_per_step * jj + which], 0, 0))

    page_specs = [page_spec(w) for w in range(pages_per_step)]
    stats = pltpu.VMEM((rows, LANES), F32)
    grid_spec = pltpu.PrefetchScalarGridSpec(
        num_scalar_prefetch=1,
        grid=(n_seq, n_pages // pages_per_step),
        in_specs=[new_spec, pl.BlockSpec((rows, 1), lambda b, jj, pt: (0, 0))] + page_specs + page_specs
                 + [new_spec, new_spec],
        out_specs=new_spec,
        scratch_shapes=[pltpu.VMEM((D_MODEL, rows), F32), pltpu.VMEM((rows, D_MODEL), BF16),
                        pltpu.VMEM((n_past, rows), F32), stats, stats,
                        pltpu.VMEM((n_past, rows, D_MODEL), F32)],
    )
    return pl.pallas_call(
        functools.partial(_attend_sample_body, dec_seq, n_past, past_len),
        grid_spec=grid_spec,
        out_shape=jax.ShapeDtypeStruct((n_seq, dec_seq, D_MODEL), F32),
        compiler_params=_params("arbitrary", "arbitrary"),
        name="moba_attend_sample",
    )(page_table, q, slope_rows, *([cache_kt] * pages_per_step), *([cache_vt] * pages_per_step), k_new, v_new)


def _out_body(o_ref, sz2_ref, x1_ref, w_ref, g_ref, y_ref):
    y_ref[...] = _out_rows(o_ref[...], sz2_ref[...], x1_ref[...], w_ref, g_ref[...])


def _out(o, sz2, x1, w_out_bf, final_g, rows_per_step):
    n_rows = o.shape[0]
    row_spec = pl.BlockSpec((rows_per_step, D_MODEL), lambda i: (i, 0))
    return pl.pallas_call(
        _out_body,
        grid=(n_rows // rows_per_step,),
        in_specs=[row_spec, row_spec, row_spec, _const_spec((D_MODEL, D_MODEL)), _const_spec((1, D_MODEL))],
        out_specs=row_spec,
        out_shape=jax.ShapeDtypeStruct((n_rows, D_MODEL), F32),
        compiler_params=_params("arbitrary"),
        name="out",
    )(o, sz2, x1, w_out_bf, final_g)


_ROWS_PER_STEP = 256
_OUT_ROWS_PER_STEP = 512


def kernel(x_prompt, x_sample, state_conv, cache_k, cache_v, page_table, a_norm_g, a_w_in, a_conv_w, a_conv_b,
           a_ln_g, a_ln_b, a_w_out, kv_norm_g, w_kv, b_norm_g, b_w_in, b_w_out, final_norm_g):
    n_seq, seq, _ = x_prompt.shape
    n_dec, dec_seq, _ = x_sample.shape
    assert a_w_in.shape[0] == 1 and b_w_in.shape[0] == 1, "one conv layer and one attention layer"
    assert seq % MOBA_BLOCK == 0 and seq >= HIST

    vec = lambda a: a.reshape(1, D_MODEL)
    a_g, cb, ln_g, ln_b = vec(a_norm_g[0]), vec(a_conv_b[0]), vec(a_ln_g[0]), vec(a_ln_b[0])
    kv_g, b_g, fin_g = vec(kv_norm_g), vec(b_norm_g[0]), vec(final_norm_g)
    cw = a_conv_w[0]
    w_in_bf, w_out_bf = a_w_in[0].astype(BF16), a_w_out[0].astype(BF16)
    w_kv_bf, w_kvt_bf = w_kv.astype(BF16), w_kv.T.astype(BF16)
    bw_q_bf, bw_z_bf = b_w_in[0, :, :D_MODEL].astype(BF16), b_w_in[0, :, D_MODEL:].astype(BF16)
    bw_out_bf = b_w_out[0].astype(BF16)
    slopes = jnp.exp2(-8.0 * jnp.arange(1, N_HEADS + 1, dtype=F32) / N_HEADS)

    xp = x_prompt.reshape(n_seq * seq, D_MODEL)
    glu_p, sz_p = _inproj(xp, a_g, w_in_bf, _OUT_ROWS_PER_STEP)
    glu_p3 = glu_p.reshape(n_seq, seq, D_MODEL)
    x1_p, sz2_p, qt_p, kt_p, vt_p, kb_p, vtb_p, kmt_p = _mid_prompt(
        glu_p3, sz_p.reshape(n_seq, seq, D_MODEL), x_prompt,
        cw, cb, ln_g, ln_b, w_out_bf, kv_g, w_kvt_bf, b_g, bw_z_bf, bw_q_bf.T)
    kmean_p = kmt_p.transpose(0, 2, 1)[:, :seq // MOBA_BLOCK]
    y_p = _attend_prompt(qt_p, kmean_p, kb_p, vtb_p, slopes, sz2_p, x1_p, bw_out_bf, fin_g)

    n_rows = n_dec * dec_seq
    xs = x_sample.transpose(1, 0, 2).reshape(n_rows, D_MODEL)
    glu_s, sz_s = _inproj(xs, a_g, w_in_bf, _ROWS_PER_STEP)
    glu_s3 = glu_s.reshape(dec_seq, n_dec, D_MODEL)
    hist_t = state_conv[0].transpose(1, 0, 2)
    y_s, hist_new = _conv_sample(hist_t, glu_s3, cw, cb)
    x1_s, q_s, sz2_s, k_s, v_s, kt_s, vt_s = _post_sample(
        y_s.reshape(n_rows, D_MODEL), sz_s, xs, ln_g, ln_b, w_out_bf, kv_g, w_kv_bf, w_kvt_bf, b_g,
        bw_z_bf, bw_q_bf, n_dec, _ROWS_PER_STEP)
    by_seq = lambda a: a.reshape(dec_seq, n_dec, D_MODEL).transpose(1, 0, 2)
    slope_rows = jnp.repeat(slopes, dec_seq).reshape(N_HEADS * dec_seq, 1)
    pages = cache_k.shape[0]
    cache_kt = cache_k.transpose(0, 2, 3, 1).reshape(pages, D_MODEL, PAGE_SIZE)
    cache_vt = cache_v.transpose(0, 2, 3, 1).reshape(pages, D_MODEL, PAGE_SIZE)
    o_s = _attend_sample(by_seq(q_s), by_seq(k_s), by_seq(v_s), cache_kt, cache_vt, page_table, slope_rows)
    y_s = _out(o_s.astype(BF16).transpose(1, 0, 2).reshape(n_rows, D_MODEL), sz2_s, x1_s, bw_out_bf, fin_g,
               _OUT_ROWS_PER_STEP)

    heads_p = lambda a: a.reshape(n_seq, N_HEADS, HEAD_DIM, seq).transpose(0, 3, 1, 2)
    heads_s = lambda a: a.reshape(dec_seq, N_HEADS, HEAD_DIM, n_dec).transpose(3, 0, 1, 2)
    return (y_p.reshape(n_seq, seq, D_MODEL), by_seq(y_s),
            glu_p3[:, seq - HIST:][None], hist_new.transpose(1, 0, 2)[None],
            heads_p(kt_p), heads_p(vt_p), heads_s(kt_s), heads_s(vt_s))
```

```python
import functools
import math

import jax
import jax.numpy as jnp
from jax import lax
from jax.experimental import pallas as pl
from jax.experimental.pallas import tpu as pltpu

D_MODEL = 1024
N_HEADS = 16
HEAD_DIM = 64
CONV_WIDTH = 31
HIST = CONV_WIDTH - 1
MOBA_BLOCK = 256
MOBA_TOPK = 3
PAGE_SIZE = 128
NORM_EPS = 1e-6
MASK_VALUE = -1e30

LANES = 128
SUBLANES = 8
BF16_ROWS = 16
VMEM_LIMIT = 56 * 1024 * 1024
LOG2E = math.log2(math.e)

F32 = jnp.float32
BF16 = jnp.bfloat16


def _idiv(x, n):
    assert n & (n - 1) == 0
    return lax.shift_right_logical(x, n.bit_length() - 1)


def _imod(x, n):
    assert n & (n - 1) == 0
    return lax.bitwise_and(x, n - 1)


def _sigmoid(x):
    return 1.0 / (1.0 + jnp.exp(-x))


def _dot(a, b, precision=None):
    return jnp.dot(a, b, preferred_element_type=F32, precision=precision)


def _dot_nt(a, b):
    return lax.dot_general(a, b, (((1,), (1,)), ((), ())), preferred_element_type=F32)


def _params(*semantics):
    return pltpu.CompilerParams(dimension_semantics=semantics,
                                vmem_limit_bytes=VMEM_LIMIT)


def _const_spec(shape):
    zeros = (0,) * len(shape)
    return pl.BlockSpec(shape, lambda *_: zeros, pipeline_mode=pl.Buffered(1))


def _out_rows(o_bf, sz2_bf, x1, w_ref, final_g):
    x2 = x1 + _dot(o_bf * sz2_bf, w_ref[...])
    return x2 * lax.rsqrt(jnp.mean(x2 * x2, axis=-1, keepdims=True) + NORM_EPS) * final_g


def _rank_is_top(scores, valid, n_top):
    n = scores.shape[0]
    row = lax.broadcasted_iota(jnp.int32, scores.shape, 0).astype(F32)
    left = jnp.where(valid, scores, MASK_VALUE)
    picked = jnp.zeros(scores.shape, F32)
    for _ in range(n_top):
        best = jnp.max(left, axis=0, keepdims=True)
        first = jnp.min(jnp.where(left == best, row, float(n)), axis=0, keepdims=True)
        hit = row == first
        picked = jnp.where(hit, 1.0, picked)
        left = jnp.where(hit, -jnp.inf, left)
    return jnp.where(valid, picked, 0.0) > 0.5


def _inproj_body(x_ref, g_ref, w_ref, glu_ref, sz_ref):
    x = x_ref[...]
    h = x * lax.rsqrt(jnp.mean(x * x, axis=-1, keepdims=True) + NORM_EPS) * g_ref[...]
    hb = h.astype(BF16)
    a = _dot(hb, w_ref[:, 0:D_MODEL])
    b = _dot(hb, w_ref[:, D_MODEL:2 * D_MODEL])
    z = _dot(hb, w_ref[:, 2 * D_MODEL:3 * D_MODEL])
    glu_ref[...] = a * _sigmoid(b)
    sz_ref[...] = z * _sigmoid(z)


def _inproj(x, norm_g, w_in_bf, rows_per_step):
    n_rows = x.shape[0]
    row_spec = pl.BlockSpec((rows_per_step, D_MODEL), lambda i: (i, 0))
    return pl.pallas_call(
        _inproj_body,
        grid=(n_rows // rows_per_step,),
        in_specs=[row_spec, _const_spec((1, D_MODEL)), _const_spec((D_MODEL, 3 * D_MODEL))],
        out_specs=[row_spec, row_spec],
        out_shape=[jax.ShapeDtypeStruct((n_rows, D_MODEL), F32)] * 2,
        compiler_params=_params("arbitrary"),
        name="inproj",
    )(x, norm_g, w_in_bf)


def _post_conv(y, sz, x, ln_g, ln_b, w_out_ref, kv_g, b_g, bw_z_ref):
    mu = jnp.mean(y, axis=-1, keepdims=True)
    yc = y - mu
    var = jnp.mean(yc * yc, axis=-1, keepdims=True)
    ln = yc * lax.rsqrt(var + NORM_EPS) * ln_g + ln_b
    u = ln * _sigmoid(ln) * sz
    x1 = x + _dot(u.astype(BF16), w_out_ref[...])
    xn = x1 * lax.rsqrt(jnp.mean(x1 * x1, axis=-1, keepdims=True) + NORM_EPS)
    xb = (xn * b_g).astype(BF16)
    z2 = _dot(xb, bw_z_ref[...])
    return x1, z2 * _sigmoid(z2), (xn * kv_g).astype(BF16), xb


_MID_ROWS = MOBA_BLOCK
_CARRY = 32
_CONV_ROWS = 128
_TAP0 = _CARRY - HIST
_SHIFT_ROWS = _CARRY + _MID_ROWS - SUBLANES


def _mid_prompt_body(glu_ref, sz_ref, x_ref, cw_ref, cb_ref, lng_ref, lnb_ref, wout_ref,
                     kvg_ref, wkvt_ref, bg_ref, bwz_ref, bwqt_ref,
                     x1_ref, sz2_ref, qt_ref, kt_ref, vt_ref, kb_ref, vtb_ref, kmt_ref,
                     full_ref, shift_ref, y_ref):
    t = pl.program_id(1)

    @pl.when(t == 0)
    def _():
        full_ref[0:_CARRY, :] = jnp.zeros((_CARRY, D_MODEL), F32)

    @pl.when(t > 0)
    def _():
        full_ref[0:_CARRY, :] = full_ref[_MID_ROWS:_MID_ROWS + _CARRY, :]

    full_ref[_CARRY:_CARRY + _MID_ROWS, :] = glu_ref[0]

    def lane_group(c, carry):
        off = pl.multiple_of(c * LANES, LANES)
        for s in range(1, SUBLANES):
            shift_ref[s - 1] = full_ref[pl.ds(s, _SHIFT_ROWS), pl.ds(off, LANES)]
        for rc in range(_MID_ROWS // _CONV_ROWS):
            acc = jnp.zeros((_CONV_ROWS, LANES), F32)
            for s in range(SUBLANES):
                taps = [w for w in range(CONV_WIDTH) if (_TAP0 + w) % SUBLANES == s]
                span = (_TAP0 + taps[-1]) // SUBLANES * SUBLANES + _CONV_ROWS
                if s == 0:
                    rows = full_ref[pl.ds(rc * _CONV_ROWS, span), pl.ds(off, LANES)]
                else:
                    rows = shift_ref[s - 1, pl.ds(rc * _CONV_ROWS, span), :]
                for w in taps:
                    a = (_TAP0 + w) // SUBLANES * SUBLANES
                    acc = acc + rows[a:a + _CONV_ROWS] * cw_ref[w:w + 1, pl.ds(off, LANES)]
            y_ref[pl.ds(rc * _CONV_ROWS, _CONV_ROWS), pl.ds(off, LANES)] = acc + cb_ref[:, pl.ds(off, LANES)]
        return carry

    lax.fori_loop(0, D_MODEL // LANES, lane_group, 0)

    x1, sz2, xkv, xb = _post_conv(y_ref[...], sz_ref[0], x_ref[0], lng_ref[...], lnb_ref[...], wout_ref,
                                  kvg_ref[...], bg_ref[...], bwz_ref)
    x1_ref[0] = x1
    sz2_ref[0] = sz2.astype(BF16)
    qt_ref[0] = _dot_nt(bwqt_ref[...], xb)
    kvt = _dot_nt(wkvt_ref[...], xkv)
    kt = kvt[:D_MODEL]
    vt = kvt[D_MODEL:]
    kt_ref[0] = kt
    vt_ref[0] = vt
    vtb_ref[0] = vt.astype(BF16)
    kb_ref[0] = kt.T.astype(BF16)
    mean_col = jnp.sum(kt, axis=1, keepdims=True) * (1.0 / MOBA_BLOCK)
    lane = lax.broadcasted_iota(jnp.int32, (D_MODEL, LANES), 1)

    @pl.when(t == 0)
    def _():
        kmt_ref[0] = jnp.where(lane == 0, mean_col, 0.0)

    @pl.when(t > 0)
    def _():
        kmt_ref[0] = jnp.where(lane == t, mean_col, kmt_ref[0])


def _mid_prompt(glu, sz, x, cw, cb, ln_g, ln_b, w_out_bf, kv_g, w_kvt_bf, b_g, bw_z_bf, bw_qt_bf):
    n_seq, seq, _ = x.shape
    n_tiles = seq // _MID_ROWS
    assert n_tiles <= LANES
    tile = pl.BlockSpec((1, _MID_ROWS, D_MODEL), lambda b, t: (b, t, 0))
    tile_t = pl.BlockSpec((1, D_MODEL, _MID_ROWS), lambda b, t: (b, 0, t))
    vec = _const_spec((1, D_MODEL))
    square = _const_spec((D_MODEL, D_MODEL))
    f32_rows = jax.ShapeDtypeStruct((n_seq, seq, D_MODEL), F32)
    f32_t = jax.ShapeDtypeStruct((n_seq, D_MODEL, seq), F32)
    return pl.pallas_call(
        _mid_prompt_body,
        grid=(n_seq, n_tiles),
        in_specs=[tile, tile, tile, _const_spec((CONV_WIDTH, D_MODEL)), vec, vec, vec,
                  square, vec, _const_spec((2 * D_MODEL, D_MODEL)), vec, square, square],
        out_specs=[tile, tile, tile_t, tile_t, tile_t, tile, tile_t,
                   pl.BlockSpec((1, D_MODEL, LANES), lambda b, t: (b, 0, 0))],
        out_shape=[f32_rows, jax.ShapeDtypeStruct((n_seq, seq, D_MODEL), BF16), f32_t, f32_t, f32_t,
                   jax.ShapeDtypeStruct((n_seq, seq, D_MODEL), BF16),
                   jax.ShapeDtypeStruct((n_seq, D_MODEL, seq), BF16),
                   jax.ShapeDtypeStruct((n_seq, D_MODEL, LANES), F32)],
        scratch_shapes=[pltpu.VMEM((_CARRY + _MID_ROWS, D_MODEL), F32),
                        pltpu.VMEM((SUBLANES - 1, _SHIFT_ROWS, LANES), F32),
                        pltpu.VMEM((_MID_ROWS, D_MODEL), F32)],
        compiler_params=_params("arbitrary", "arbitrary"),
        name="mid_prompt",
    )(glu, sz, x, cw, cb, ln_g, ln_b, w_out_bf, kv_g, w_kvt_bf, b_g, bw_z_bf, bw_qt_bf)


def _conv_sample_body(dec_seq, hist_ref, glu_ref, cw_ref, cb_ref, y_ref, hist_out_ref):
    def tap(r):
        return hist_ref[r] if r < HIST else glu_ref[r - HIST]

    for t in range(dec_seq):
        acc = tap(t) * cw_ref[0:1, :]
        for w in range(1, CONV_WIDTH):
            acc = acc + tap(t + w) * cw_ref[w:w + 1, :]
        y_ref[t] = acc + cb_ref[...]
    for r in range(HIST):
        hist_out_ref[r] = tap(r + dec_seq)


def _conv_sample(hist_t, glu_t, cw, cb):
    dec_seq, n_seq, _ = glu_t.shape
    lane_blk = lambda rows: pl.BlockSpec((rows, n_seq, LANES), lambda c: (0, 0, c))
    return pl.pallas_call(
        functools.partial(_conv_sample_body, dec_seq),
        grid=(D_MODEL // LANES,),
        in_specs=[lane_blk(HIST), lane_blk(dec_seq),
                  pl.BlockSpec((CONV_WIDTH, LANES), lambda c: (0, c)),
                  pl.BlockSpec((1, LANES), lambda c: (0, c))],
        out_specs=[lane_blk(dec_seq), lane_blk(HIST)],
        out_shape=[jax.ShapeDtypeStruct((dec_seq, n_seq, D_MODEL), F32),
                   jax.ShapeDtypeStruct((HIST, n_seq, D_MODEL), F32)],
        compiler_params=_params("arbitrary"),
        name="conv_sample",
    )(hist_t, glu_t, cw, cb)


def _post_sample_body(t_per_step, n_seq, y_ref, sz_ref, x_ref, lng_ref, lnb_ref, wout_ref,
                      kvg_ref, wkv_ref, wkvt_ref, bg_ref, bwz_ref, bwq_ref,
                      x1_ref, q_ref, sz2_ref, k_ref, v_ref, kt_ref, vt_ref):
    x1, sz2, xkv, xb = _post_conv(y_ref[...], sz_ref[...], x_ref[...], lng_ref[...], lnb_ref[...], wout_ref,
                                  kvg_ref[...], bg_ref[...], bwz_ref)
    x1_ref[...] = x1
    sz2_ref[...] = sz2.astype(BF16)
    q_ref[...] = _dot(xb, bwq_ref[...])
    kv = _dot(xkv, wkv_ref[...])
    k_ref[...] = kv[:, :D_MODEL]
    v_ref[...] = kv[:, D_MODEL:]
    kvt = _dot_nt(wkvt_ref[...], xkv)
    for i in range(t_per_step):
        kt_ref[i] = kvt[:D_MODEL, i * n_seq:(i + 1) * n_seq]
        vt_ref[i] = kvt[D_MODEL:, i * n_seq:(i + 1) * n_seq]


def _post_sample(y, sz, x, ln_g, ln_b, w_out_bf, kv_g, w_kv_bf, w_kvt_bf, b_g, bw_z_bf, bw_q_bf,
                 n_seq, rows_per_step):
    n_rows = y.shape[0]
    dec_seq = n_rows // n_seq
    t_per_step = rows_per_step // n_seq
    assert n_seq % LANES == 0 and rows_per_step % n_seq == 0
    row_spec = pl.BlockSpec((rows_per_step, D_MODEL), lambda i: (i, 0))
    t_spec = pl.BlockSpec((t_per_step, D_MODEL, n_seq), lambda i: (i, 0, 0))
    vec = _const_spec((1, D_MODEL))
    square = _const_spec((D_MODEL, D_MODEL))
    f32_rows = jax.ShapeDtypeStruct((n_rows, D_MODEL), F32)
    f32_t = jax.ShapeDtypeStruct((dec_seq, D_MODEL, n_seq), F32)
    return pl.pallas_call(
        functools.partial(_post_sample_body, t_per_step, n_seq),
        grid=(n_rows // rows_per_step,),
        in_specs=[row_spec, row_spec, row_spec, vec, vec, square, vec,
                  _const_spec((D_MODEL, 2 * D_MODEL)), _const_spec((2 * D_MODEL, D_MODEL)),
                  vec, square, square],
        out_specs=[row_spec] * 5 + [t_spec] * 2,
        out_shape=[f32_rows, f32_rows, jax.ShapeDtypeStruct((n_rows, D_MODEL), BF16), f32_rows, f32_rows]
                  + [f32_t] * 2,
        compiler_params=_params("arbitrary"),
        name="post_sample",
    )(y, sz, x, ln_g, ln_b, w_out_bf, kv_g, w_kv_bf, w_kvt_bf, b_g, bw_z_bf, bw_q_bf)


_HEADS_PER_STEP = 16
_KEY_ROWS = 128
_PV_ROWS = HEAD_DIM + BF16_ROWS


def _softmax_probs(t, shift, m_ref, h, first):
    m_cur = jnp.max(t, axis=0, keepdims=True) - shift
    if first:
        m_new, alpha = m_cur, None
    else:
        m_prev = m_ref[h, 0:1, :]
        m_new = jnp.maximum(m_prev, m_cur)
        alpha = jnp.exp2(m_prev - m_new)
    m_ref[h] = jnp.broadcast_to(m_new, (SUBLANES, t.shape[1]))
    return jnp.exp2(t - (m_new + shift)).astype(BF16), alpha


def _attend_prompt_body(n_blocks, slopes_ref, qt_ref, km_ref, k_ref, vt_ref, sz2_ref, x1_ref, wout_ref, fing_ref,
                        y_ref, qaug_ref, m_ref, acc_ref, o_scr):
    i = pl.program_id(1)
    heads = range(_HEADS_PER_STEP)
    key = lax.broadcasted_iota(jnp.int32, (MOBA_BLOCK, MOBA_BLOCK), 0)
    qry = lax.broadcasted_iota(jnp.int32, (MOBA_BLOCK, MOBA_BLOCK), 1)
    blk = lax.broadcasted_iota(jnp.int32, (n_blocks, MOBA_BLOCK), 0)
    feat_head = _idiv(lax.broadcasted_iota(jnp.int32, (LANES, MOBA_BLOCK), 0), HEAD_DIM)
    lane_head = _idiv(lax.broadcasted_iota(jnp.int32, (n_blocks, LANES), 1), HEAD_DIM)
    own0 = pl.multiple_of(i * MOBA_BLOCK, MOBA_BLOCK)
    slope2 = [slopes_ref[h] * LOG2E for h in heads]
    aug_pad = jnp.zeros((LANES - n_blocks - SUBLANES, MOBA_BLOCK), F32)
    piece_row = lax.broadcasted_iota(jnp.int32, (SUBLANES, MOBA_BLOCK), 0)
    lane = lax.broadcasted_iota(jnp.int32, (_KEY_ROWS, LANES), 1)
    key_row = lax.broadcasted_iota(jnp.int32, (_KEY_ROWS, LANES), 0)
    pos_cols = []
    for c in range(MOBA_BLOCK // _KEY_ROWS):
        pos = (key_row + c * _KEY_ROWS).astype(F32)
        pos_cols.append(jnp.where(lane == n_blocks, pos, jnp.where(lane == n_blocks + 1, pos,
                                  jnp.where(lane == n_blocks + 2, pos, 0.0))))

    def slope_rows(h):
        full = jnp.full((SUBLANES, MOBA_BLOCK), slope2[h], F32)
        hi = full.astype(BF16).astype(F32)
        mid = (full - hi).astype(BF16).astype(F32)
        lo = (full - hi - mid).astype(BF16).astype(F32)
        return jnp.where(piece_row == 0, hi, jnp.where(piece_row == 1, mid, jnp.where(piece_row == 2, lo, 0.0)))

    def pair_rows(h):
        return slice((h // 2) * LANES, (h // 2 + 1) * LANES)

    def v_aug(h, off, width=MOBA_BLOCK):
        return jnp.concatenate([vt_ref[0, h * HEAD_DIM:(h + 1) * HEAD_DIM, pl.ds(off, width)],
                                jnp.ones((BF16_ROWS, width), BF16)], axis=0)

    pair_scores = []
    for pair in range(_HEADS_PER_STEP // 2):
        km_pair = km_ref[0, :, pair_rows(2 * pair)]
        km_heads = jnp.concatenate([jnp.where(lane_head == e, km_pair, 0.0) for e in range(2)], axis=0)
        pair_scores.append(_dot(km_heads, qt_ref[0, pair_rows(2 * pair), :], precision=lax.Precision.HIGHEST))
    for h in heads:
        scores_h = pair_scores[h // 2][(h % 2) * n_blocks:(h % 2 + 1) * n_blocks]
        chosen = _rank_is_top(scores_h, blk < i, MOBA_TOPK)
        bias = jnp.where(chosen, 0.0, MASK_VALUE)
        q_h = jnp.where(feat_head == h % 2, qt_ref[0, pair_rows(h), :] * (HEAD_DIM ** -0.5 * LOG2E), 0.0)
        qaug_ref[h] = jnp.concatenate([q_h, bias, slope_rows(h), aug_pad], axis=0).astype(BF16)

    own_cols = jnp.concatenate(pos_cols, axis=0).astype(BF16)
    s_own = [_dot(jnp.concatenate([k_ref[0, pl.ds(own0, MOBA_BLOCK), pair_rows(h)], own_cols], axis=1), qaug_ref[h])
             for h in heads]
    p_own = [_softmax_probs(jnp.where(qry >= key, s_own[h], MASK_VALUE), 0.0, m_ref, h, first=True)[0]
             for h in heads]
    for h in heads:
        acc_ref[h] = _dot(v_aug(h, own0), p_own[h])

    def past_block(j, carry):
        off = pl.multiple_of(j * MOBA_BLOCK, MOBA_BLOCK)
        cols = [jnp.where(lane == j, 1.0, pos).astype(BF16) for pos in pos_cols]
        gap = ((i - j) * MOBA_BLOCK).astype(F32)
        parts = [(h, c) for h in heads for c in range(MOBA_BLOCK // _KEY_ROWS)]
        s = [_dot(jnp.concatenate([k_ref[0, pl.ds(off + c * _KEY_ROWS, _KEY_ROWS), pair_rows(h)], cols[c]], axis=1),
                  qaug_ref[h]) for h, c in parts]
        probs = [_softmax_probs(s[n], slope2[h] * gap, m_ref, h, first=False) for n, (h, c) in enumerate(parts)]
        for n, (h, c) in enumerate(parts):
            p, alpha = probs[n]
            acc_ref[h] = alpha * acc_ref[h] + _dot(v_aug(h, off + c * _KEY_ROWS, _KEY_ROWS), p)
        return carry

    lax.fori_loop(0, i, past_block, 0)

    for pair in range(_HEADS_PER_STEP // 2):
        outs = []
        for e in range(2):
            acc = acc_ref[2 * pair + e]
            outs.append(acc[:HEAD_DIM] / acc[HEAD_DIM:HEAD_DIM + 1])
        o_scr[:, pair * LANES:(pair + 1) * LANES] = jnp.concatenate(outs, axis=0).T.astype(BF16)

    y_ref[0] = _out_rows(o_scr[...], sz2_ref[0], x1_ref[0], wout_ref, fing_ref[...])


def _attend_prompt(qt, kmean, k_bf, vt_bf, slopes, sz2, x1, w_out_bf, final_g):
    n_seq, _, seq = qt.shape
    n_blocks = seq // MOBA_BLOCK
    assert n_blocks <= LANES and n_blocks % SUBLANES == 0 and 2 * HEAD_DIM == LANES
    assert _HEADS_PER_STEP == N_HEADS, "the fused out stage needs every head of a query block in one step"
    rows = pl.BlockSpec((1, MOBA_BLOCK, D_MODEL), lambda b, i: (b, i, 0))
    whole = lambda shape: pl.BlockSpec(shape, lambda b, i: (b, 0, 0))
    return pl.pallas_call(
        functools.partial(_attend_prompt_body, n_blocks),
        grid=(n_seq, n_blocks),
        in_specs=[pl.BlockSpec(memory_space=pltpu.SMEM),
                  pl.BlockSpec((1, D_MODEL, MOBA_BLOCK), lambda b, i: (b, 0, i)),
                  whole((1, n_blocks, D_MODEL)), whole((1, seq, D_MODEL)), whole((1, D_MODEL, seq)),
                  rows, rows, _const_spec((D_MODEL, D_MODEL)), _const_spec((1, D_MODEL))],
        out_specs=rows,
        out_shape=jax.ShapeDtypeStruct((n_seq, seq, D_MODEL), F32),
        scratch_shapes=[pltpu.VMEM((_HEADS_PER_STEP, 2 * LANES, MOBA_BLOCK), BF16),
                        pltpu.VMEM((_HEADS_PER_STEP, SUBLANES, MOBA_BLOCK), F32),
                        pltpu.VMEM((_HEADS_PER_STEP, _PV_ROWS, MOBA_BLOCK), F32),
                        pltpu.VMEM((MOBA_BLOCK, D_MODEL), BF16)],
        compiler_params=_params("arbitrary", "arbitrary"),
        name="moba_attend_prompt",
    )(slopes, qt, kmean, k_bf, vt_bf, sz2, x1, w_out_bf, final_g)


_BLOCKS_PER_STEP = 8
_PAGES_PER_BLOCK = MOBA_BLOCK // PAGE_SIZE


def _attend_sample_body(dec_seq, n_past, past_len, pt_ref, q_ref, slope_ref, *refs):
    n_pg = _BLOCKS_PER_STEP * _PAGES_PER_BLOCK
    k_pages, v_pages = refs[:n_pg], refs[n_pg:2 * n_pg]
    kn_ref, vn_ref, o_ref, qdt_ref, qdb_ref, sc_ref, m_ref, l_ref, acc_ref = refs[2 * n_pg:]
    jj = pl.program_id(1)
    rows = N_HEADS * dec_seq
    lane = lax.broadcasted_iota(jnp.int32, (rows, LANES), 1)
    slope = slope_ref[...]

    @pl.when(jj == 0)
    def _():
        q = q_ref[0]
        q_rows = jnp.concatenate([q] * N_HEADS, axis=0)
        r_head = _idiv(lax.broadcasted_iota(jnp.int32, (rows, D_MODEL), 0), dec_seq)
        l_head = _idiv(lax.broadcasted_iota(jnp.int32, (rows, D_MODEL), 1), HEAD_DIM)
        q_diag = jnp.where(r_head == l_head, q_rows, 0.0)
        qdt_ref[...] = q_diag.T
        qdb_ref[...] = (q_diag * (HEAD_DIM ** -0.5)).astype(BF16)
        m_ref[...] = jnp.zeros((rows, LANES), F32)
        l_ref[...] = jnp.zeros((rows, LANES), F32)

    q_off = _imod(lax.broadcasted_iota(jnp.int32, (rows, MOBA_BLOCK), 0), dec_seq)
    k_off = lax.broadcasted_iota(jnp.int32, (rows, MOBA_BLOCK), 1)
    blocks = range(_BLOCKS_PER_STEP)
    js = [jj * _BLOCKS_PER_STEP + b for b in blocks]

    def block_of(page_refs, b):
        return jnp.concatenate([r[0] for r in page_refs[b * _PAGES_PER_BLOCK:(b + 1) * _PAGES_PER_BLOCK]], axis=1)

    s = []
    for b in blocks:
        kt_blk = block_of(k_pages, b)
        s.append(_dot(qdb_ref[...], kt_blk.astype(BF16)))
        k_mean = jnp.sum(kt_blk, axis=1, keepdims=True) * (1.0 / MOBA_BLOCK)
        sc_ref[pl.ds(js[b], 1), :] = jnp.sum(qdt_ref[...] * k_mean, axis=0, keepdims=True)
    p = []
    m_all, l_all = m_ref[...], l_ref[...]
    for b in blocks:
        t = s[b] - slope * (q_off - k_off + (past_len - js[b] * MOBA_BLOCK)).astype(F32)
        m_j = jnp.max(t, axis=1, keepdims=True)
        e = jnp.exp(t - m_j)
        m_all = jnp.where(lane == js[b], m_j, m_all)
        l_all = jnp.where(lane == js[b], jnp.sum(e, axis=1, keepdims=True), l_all)
        p.append(e.astype(BF16))
    m_ref[...] = m_all
    l_ref[...] = l_all
    for b in blocks:
        acc_ref[js[b]] = _dot_nt(p[b], block_of(v_pages, b).astype(BF16))

    @pl.when(jj == n_past // _BLOCKS_PER_STEP - 1)
    def _():
        pad = jnp.zeros((LANES - dec_seq, D_MODEL), F32)
        k_new = jnp.concatenate([kn_ref[0], pad], axis=0).astype(BF16)
        v_new = jnp.concatenate([vn_ref[0], pad], axis=0).astype(BF16)
        s_own = _dot_nt(qdb_ref[...], k_new)
        d_own = _imod(lax.broadcasted_iota(jnp.int32, (rows, LANES), 0), dec_seq) - lane
        t_own = jnp.where(d_own >= 0, s_own - slope * d_own.astype(F32), MASK_VALUE)
        m_own = jnp.max(t_own, axis=1, keepdims=True)
        p_own = jnp.exp(t_own - m_own)
        l_own = jnp.sum(p_own, axis=1, keepdims=True)
        acc_own = _dot(p_own.astype(BF16), v_new)

        always = lax.broadcasted_iota(jnp.int32, (n_past, rows), 0) >= 0
        chosen_t = jnp.where(_rank_is_top(sc_ref[...], always, min(MOBA_TOPK, n_past)), 1.0, 0.0)
        chosen = jnp.concatenate([chosen_t, jnp.zeros((LANES - n_past, rows), F32)], axis=0).T > 0.5

        m_all = m_ref[...]
        m_top = jnp.maximum(jnp.max(jnp.where(chosen, m_all, MASK_VALUE), axis=1, keepdims=True), m_own)
        w_all = jnp.where(chosen, jnp.exp(m_all - m_top), 0.0)
        w_own = jnp.exp(m_own - m_top)
        denom = jnp.sum(w_all * l_ref[...], axis=1, keepdims=True) + w_own * l_own
        num = w_own * acc_own
        for b in range(n_past):
            num = num + w_all[:, b:b + 1] * acc_ref[b]
        out = num / denom
        l_head = _idiv(lax.broadcasted_iota(jnp.int32, (dec_seq, D_MODEL), 1), HEAD_DIM)
        res = jnp.zeros((dec_seq, D_MODEL), F32)
        for h in range(N_HEADS):
            res = jnp.where(l_head == h, out[h * dec_seq:(h + 1) * dec_seq, :], res)
        o_ref[0] = res


def _attend_sample(q, k_new, v_new, cache_kt, cache_vt, page_table, slope_rows):
    n_seq, dec_seq, _ = q.shape
    n_pages = page_table.shape[1]
    pages_per_step = _BLOCKS_PER_STEP * _PAGES_PER_BLOCK
    assert n_pages % pages_per_step == 0 and dec_seq % SUBLANES == 0
    n_past = n_pages // _PAGES_PER_BLOCK
    past_len = n_pages * PAGE_SIZE
    rows = N_HEADS * dec_seq
    assert n_past % SUBLANES == 0 and n_past <= LANES and dec_seq <= LANES and rows % LANES == 0
    new_spec = pl.BlockSpec((1, dec_seq, D_MODEL), lambda b, jj, pt: (b, 0, 0))

    def page_spec(which):
        return pl.BlockSpec((1, D_MODEL, PAGE_SIZE), lambda b, jj, pt: (pt[b, pages_per_step * jj + which], 0, 0))

    page_specs = [page_spec(w) for w in range(pages_per_step)]
    stats = pltpu.VMEM((rows, LANES), F32)
    grid_spec = pltpu.PrefetchScalarGridSpec(
        num_scalar_prefetch=1,
        grid=(n_seq, n_pages // pages_per_step),
        in_specs=[new_spec, pl.BlockSpec((rows, 1), lambda b, jj, pt: (0, 0))] + page_specs + page_specs
                 + [new_spec, new_spec],
        out_specs=new_spec,
        scratch_shapes=[pltpu.VMEM((D_MODEL, rows), F32), pltpu.VMEM((rows, D_MODEL), BF16),
                        pltpu.VMEM((n_past, rows), F32), stats, stats,
                        pltpu.VMEM((n_past, rows, D_MODEL), F32)],
    )
    return pl.pallas_call(
        functools.partial(_attend_sample_body, dec_seq, n_past, past_len),
        grid_spec=grid_spec,
        out_shape=jax.ShapeDtypeStruct((n_seq, dec_seq, D_MODEL), F32),
        compiler_params=_params("arbitrary", "arbitrary"),
        name="moba_attend_sample",
    )(page_table, q, slope_rows, *([cache_kt] * pages_per_step), *([cache_vt] * pages_per_step), k_new, v_new)


def _out_body(o_ref, sz2_ref, x1_ref, w_ref, g_ref, y_ref):
    y_ref[...] = _out_rows(o_ref[...], sz2_ref[...], x1_ref[...], w_ref, g_ref[...])


def _out(o, sz2, x1, w_out_bf, final_g, rows_per_step):
    n_rows = o.shape[0]
    row_spec = pl.BlockSpec((rows_per_step, D_MODEL), lambda i: (i, 0))
    return pl.pallas_call(
        _out_body,
        grid=(n_rows // rows_per_step,),
        in_specs=[row_spec, row_spec, row_spec, _const_spec((D_MODEL, D_MODEL)), _const_spec((1, D_MODEL))],
        out_specs=row_spec,
        out_shape=jax.ShapeDtypeStruct((n_rows, D_MODEL), F32),
        compiler_params=_params("arbitrary"),
        name="out",
    )(o, sz2, x1, w_out_bf, final_g)


_ROWS_PER_STEP = 256
_OUT_ROWS_PER_STEP = 512


def kernel(x_prompt, x_sample, state_conv, cache_k, cache_v, page_table, a_norm_g, a_w_in, a_conv_w, a_conv_b,
           a_ln_g, a_ln_b, a_w_out, kv_norm_g, w_kv, b_norm_g, b_w_in, b_w_out, final_norm_g):
    n_seq, seq, _ = x_prompt.shape
    n_dec, dec_seq, _ = x_sample.shape
    assert a_w_in.shape[0] == 1 and b_w_in.shape[0] == 1, "one conv layer and one attention layer"
    assert seq % MOBA_BLOCK == 0 and seq >= HIST

    vec = lambda a: a.reshape(1, D_MODEL)
    a_g, cb, ln_g, ln_b = vec(a_norm_g[0]), vec(a_conv_b[0]), vec(a_ln_g[0]), vec(a_ln_b[0])
    kv_g, b_g, fin_g = vec(kv_norm_g), vec(b_norm_g[0]), vec(final_norm_g)
    cw = a_conv_w[0]
    w_in_bf, w_out_bf = a_w_in[0].astype(BF16), a_w_out[0].astype(BF16)
    w_kv_bf, w_kvt_bf = w_kv.astype(BF16), w_kv.T.astype(BF16)
    bw_q_bf, bw_z_bf = b_w_in[0, :, :D_MODEL].astype(BF16), b_w_in[0, :, D_MODEL:].astype(BF16)
    bw_out_bf = b_w_out[0].astype(BF16)
    slopes = jnp.exp2(-8.0 * jnp.arange(1, N_HEADS + 1, dtype=F32) / N_HEADS)

    xp = x_prompt.reshape(n_seq * seq, D_MODEL)
    glu_p, sz_p = _inproj(xp, a_g, w_in_bf, _OUT_ROWS_PER_STEP)
    glu_p3 = glu_p.reshape(n_seq, seq, D_MODEL)
    x1_p, sz2_p, qt_p, kt_p, vt_p, kb_p, vtb_p, kmt_p = _mid_prompt(
        glu_p3, sz_p.reshape(n_seq, seq, D_MODEL), x_prompt,
        cw, cb, ln_g, ln_b, w_out_bf, kv_g, w_kvt_bf, b_g, bw_z_bf, bw_q_bf.T)
    kmean_p = kmt_p.transpose(0, 2, 1)[:, :seq // MOBA_BLOCK]
    y_p = _attend_prompt(qt_p, kmean_p, kb_p, vtb_p, slopes, sz2_p, x1_p, bw_out_bf, fin_g)

    n_rows = n_dec * dec_seq
    xs = x_sample.transpose(1, 0, 2).reshape(n_rows, D_MODEL)
    glu_s, sz_s = _inproj(xs, a_g, w_in_bf, _ROWS_PER_STEP)
    glu_s3 = glu_s.reshape(dec_seq, n_dec, D_MODEL)
    hist_t = state_conv[0].transpose(1, 0, 2)
    y_s, hist_new = _conv_sample(hist_t, glu_s3, cw, cb)
    x1_s, q_s, sz2_s, k_s, v_s, kt_s, vt_s = _post_sample(
        y_s.reshape(n_rows, D_MODEL), sz_s, xs, ln_g, ln_b, w_out_bf, kv_g, w_kv_bf, w_kvt_bf, b_g,
        bw_z_bf, bw_q_bf, n_dec, _ROWS_PER_STEP)
    by_seq = lambda a: a.reshape(dec_seq, n_dec, D_MODEL).transpose(1, 0, 2)
    slope_rows = jnp.repeat(slopes, dec_seq).reshape(N_HEADS * dec_seq, 1)
    pages = cache_k.shape[0]
    cache_kt = cache_k.transpose(0, 2, 3, 1).reshape(pages, D_MODEL, PAGE_SIZE)
    cache_vt = cache_v.transpose(0, 2, 3, 1).reshape(pages, D_MODEL, PAGE_SIZE)
    o_s = _attend_sample(by_seq(q_s), by_seq(k_s), by_seq(v_s), cache_kt, cache_vt, page_table, slope_rows)
    y_s = _out(o_s.astype(BF16).transpose(1, 0, 2).reshape(n_rows, D_MODEL), sz2_s, x1_s, bw_out_bf, fin_g,
               _OUT_ROWS_PER_STEP)

    heads_p = lambda a: a.reshape(n_seq, N_HEADS, HEAD_DIM, seq).transpose(0, 3, 1, 2)
    heads_s = lambda a: a.reshape(dec_seq, N_HEADS, HEAD_DIM, n_dec).transpose(3, 0, 1, 2)
    return (y_p.reshape(n_seq, seq, D_MODEL), by_seq(y_s),
            glu_p3[:, seq - HIST:][None], hist_new.transpose(1, 0, 2)[None],
            heads_p(kt_p), heads_p(vt_p), heads_s(kt_s), heads_s(vt_s))
```
